```python
import math
import jax, jax.numpy as jnp
from jax import lax
import numpy as np

D_MODEL = 1024
BATCH = 8
SEQ = 2048
DEPTH = 2
DEC_BATCH = 32
DEC_SEQ = 1
PAST_LEN = 16384
PAGE_SIZE = 128

N_META = 16
N_A_LAYERS = DEPTH // 2
N_B_LAYERS = DEPTH - N_A_LAYERS
CONV_WIDTH = 31
D_CONV = D_MODEL
N_HEADS = 16
HEAD_DIM = D_MODEL // N_HEADS
KV_DIM = N_HEADS * HEAD_DIM
D_FF = 4 * D_MODEL
Q_BLOCK = 128
EPS = 1e-6
FORGET_BIAS_INIT = 3.0
NEG_INF = -1e30

kernel_name = "yoco_conformer_conv_fox_decoder_step"


def rms_norm(x, g):
    xf = x.astype(jnp.float32)
    y = xf * lax.rsqrt(jnp.mean(xf * xf, axis=-1, keepdims=True) + EPS)
    return (y * g.astype(jnp.float32)).astype(x.dtype)


def layer_norm(x, g, b):
    xf = x.astype(jnp.float32)
    mu = jnp.mean(xf, axis=-1, keepdims=True)
    xc = xf - mu
    y = xc * lax.rsqrt(jnp.mean(xc * xc, axis=-1, keepdims=True) + EPS)
    return (y * g.astype(jnp.float32) + b.astype(jnp.float32)).astype(x.dtype)


def conformer_conv(x, ctx, norm_g, w_in, b_in, w_dw, b_dw, ln_g, ln_b, w_out, b_out):
    hn = rms_norm(x, norm_g)
    a, gate = jnp.split(hn @ w_in + b_in, 2, axis=-1)
    u = a * jax.nn.sigmoid(gate)
    u_ext = jnp.concatenate([ctx.astype(u.dtype), u], axis=1)
    y = lax.conv_general_dilated(u_ext, w_dw[:, None, :].astype(u.dtype), window_strides=(1,),
                                 padding="VALID", dimension_numbers=("NWC", "WIO", "NWC"),
                                 feature_group_count=D_CONV) + b_dw
    y = jax.nn.silu(layer_norm(y, ln_g, ln_b))
    out = y @ w_out + b_out
    return x + out, u_ext[:, -(CONV_WIDTH - 1):]


def sq_relu_mlp(x, g, w_up, w_down):
    hn = rms_norm(x, g)
    return x + jnp.square(jax.nn.relu(hn @ w_up)) @ w_down


def shared_kv(h, kv_norm_g, w_kvf, b_f, k_norm_g):
    hn = rms_norm(h, kv_norm_g)
    p = hn @ w_kvf
    lead = h.shape[:-1]
    k = rms_norm(p[..., :KV_DIM].reshape(*lead, N_HEADS, HEAD_DIM), k_norm_g)
    v = p[..., KV_DIM:2 * KV_DIM].reshape(*lead, N_HEADS, HEAD_DIM)
    logf = jax.nn.log_sigmoid((p[..., 2 * KV_DIM:] + b_f).astype(jnp.float32))
    return k, v, logf


def fox_logits(q, k, c_q, c_k, q_idx=None, k_idx=None):
    s = jnp.einsum("bqhd,bkhd->bhqk", q, k, preferred_element_type=jnp.float32) * (HEAD_DIM ** -0.5)
    s = s + (jnp.transpose(c_q, (0, 2, 1))[..., :, None] - jnp.transpose(c_k, (0, 2, 1))[..., None, :])
    if q_idx is not None:
        s = jnp.where(k_idx[None, :] <= q_idx[:, None], s, NEG_INF)
    return s


def prompt_attend(q, k, v, logf):
    L = q.shape[1]
    c = jnp.cumsum(logf, axis=1)
    bounds = [0, N_META] + list(range(N_META + Q_BLOCK, L, Q_BLOCK)) + [L]
    outs = []
    for lo, hi in zip(bounds[:-1], bounds[1:]):
        s = fox_logits(q[:, lo:hi], k[:, :hi], c[:, lo:hi], c[:, :hi],
                       jnp.arange(lo, hi), jnp.arange(hi))
        p = jax.nn.softmax(s, axis=-1).astype(v.dtype)
        outs.append(jnp.einsum("bhqk,bkhd->bqhd", p, v[:, :hi]))
    return jnp.concatenate(outs, axis=1)


def sample_attend(q, k_new, v_new, logf_new, k_past, v_past, logf_past):
    P = k_past.shape[1]
    n = q.shape[1]
    c = jnp.cumsum(jnp.concatenate([logf_past, logf_new], axis=1), axis=1)
    c_past, c_new = c[:, :P], c[:, P:]
    outs = []
    for lo in range(0, n, Q_BLOCK):
        hi = min(lo + Q_BLOCK, n)
        s_past = fox_logits(q[:, lo:hi], k_past, c_new[:, lo:hi], c_past)
        s_new = fox_logits(q[:, lo:hi], k_new[:, :hi], c_new[:, lo:hi], c_new[:, :hi],
                           jnp.arange(lo, hi), jnp.arange(hi))
        p = jax.nn.softmax(jnp.concatenate([s_past, s_new], axis=-1), axis=-1).astype(v_new.dtype)
        o = (jnp.einsum("bhqk,bkhd->bqhd", p[..., :P], v_past)
             + jnp.einsum("bhqk,bkhd->bqhd", p[..., P:], v_new[:, :hi]))
        outs.append(o)
    return jnp.concatenate(outs, axis=1)


def setup_inputs(seed: int = 0) -> dict:
    key = jax.random.key(seed)
    ks = jax.random.split(key, 32)
    f32 = jnp.float32
    n_pages = PAST_LEN // PAGE_SIZE
    n_used = DEC_BATCH * n_pages
    n_pool = n_used + max(1, n_used // 4)

    def nrm(k, shape, scale):
        return jax.random.normal(k, shape, f32) * scale

    def gain(k, shape):
        return 1.0 + nrm(k, shape, 0.02)

    perm = jax.random.permutation(ks[0], n_pool)
    page_table = perm[:n_used].reshape(DEC_BATCH, n_pages).astype(jnp.int32)
    return {
        "x_prompt": nrm(ks[1], (BATCH, SEQ, D_MODEL), 1.0),
        "x_sample": nrm(ks[2], (DEC_BATCH, DEC_SEQ, D_MODEL), 1.0),
        "state_conv": nrm(ks[3], (N_A_LAYERS, DEC_BATCH, CONV_WIDTH - 1, D_CONV), 0.5),
        "cache_k": nrm(ks[4], (n_pool, PAGE_SIZE, N_HEADS, HEAD_DIM), 1.0),
        "cache_v": nrm(ks[5], (n_pool, PAGE_SIZE, N_HEADS, HEAD_DIM), 1.0),
        "cache_logf": jax.nn.log_sigmoid(FORGET_BIAS_INIT + nrm(ks[6], (n_pool, PAGE_SIZE, N_HEADS), 1.0)),
        "page_table": page_table,
        "meta_tokens": nrm(ks[7], (N_META, D_MODEL), 1.0),
        "a_norm_g": gain(ks[8], (N_A_LAYERS, D_MODEL)),
        "a_w_in": nrm(ks[9], (N_A_LAYERS, D_MODEL, 2 * D_CONV), D_MODEL ** -0.5),
        "a_b_in": nrm(ks[10], (N_A_LAYERS, 2 * D_CONV), 0.02),
        "a_w_dw": nrm(ks[11], (N_A_LAYERS, CONV_WIDTH, D_CONV), CONV_WIDTH ** -0.5),
        "a_b_dw": nrm(ks[12], (N_A_LAYERS, D_CONV), 0.02),
        "a_ln_g": gain(ks[13], (N_A_LAYERS, D_CONV)),
        "a_ln_b": nrm(ks[14], (N_A_LAYERS, D_CONV), 0.02),
        "a_w_out": nrm(ks[15], (N_A_LAYERS, D_CONV, D_MODEL), D_CONV ** -0.5),
        "a_b_out": nrm(ks[16], (N_A_LAYERS, D_MODEL), 0.02),
        "kv_norm_g": gain(ks[17], (D_MODEL,)),
        "w_kvf": nrm(ks[18], (D_MODEL, 2 * KV_DIM + N_HEADS), D_MODEL ** -0.5),
        "b_f": FORGET_BIAS_INIT + nrm(ks[19], (N_HEADS,), 0.1),
        "k_norm_g": gain(ks[20], (HEAD_DIM,)),
        "b_norm_g": gain(ks[21], (N_B_LAYERS, D_MODEL)),
        "b_w_q": nrm(ks[22], (N_B_LAYERS, D_MODEL, KV_DIM), D_MODEL ** -0.5),
        "q_norm_g": gain(ks[23], (N_B_LAYERS, HEAD_DIM)),
        "b_w_o": nrm(ks[24], (N_B_LAYERS, KV_DIM, D_MODEL), KV_DIM ** -0.5),
        "mlp_norm_g": gain(ks[25], (DEPTH, D_MODEL)),
        "mlp_w_up": nrm(ks[26], (DEPTH, D_MODEL, D_FF), D_MODEL ** -0.5),
        "mlp_w_down": nrm(ks[27], (DEPTH, D_FF, D_MODEL), D_FF ** -0.5),
    }


def reference(x_prompt, x_sample, state_conv, cache_k, cache_v, cache_logf, page_table,
              meta_tokens, a_norm_g, a_w_in, a_b_in, a_w_dw, a_b_dw, a_ln_g, a_ln_b, a_w_out, a_b_out,
              kv_norm_g, w_kvf, b_f, k_norm_g, b_norm_g, b_w_q, q_norm_g, b_w_o,
              mlp_norm_g, mlp_w_up, mlp_w_down):

    def trunk(h, conv_ctx, attend):
        new_ctx = []
        k = v = logf = None
        for layer in range(DEPTH):
            if layer < N_A_LAYERS:
                i = layer
                h, ctx_i = conformer_conv(h, conv_ctx[i], a_norm_g[i], a_w_in[i], a_b_in[i], a_w_dw[i],
                                          a_b_dw[i], a_ln_g[i], a_ln_b[i], a_w_out[i], a_b_out[i])
                new_ctx.append(ctx_i)
            else:
                j = layer - N_A_LAYERS
                hn = rms_norm(h, b_norm_g[j])
                q = rms_norm((hn @ b_w_q[j]).reshape(h.shape[0], h.shape[1], N_HEADS, HEAD_DIM), q_norm_g[j])
                o = attend(q, k, v, logf)
                h = h + o.reshape(h.shape[0], h.shape[1], KV_DIM) @ b_w_o[j]
            h = sq_relu_mlp(h, mlp_norm_g[layer], mlp_w_up[layer], mlp_w_down[layer])
            if layer == N_A_LAYERS - 1:
                k, v, logf = shared_kv(h, kv_norm_g, w_kvf, b_f, k_norm_g)
        return h, jnp.stack(new_ctx), k, v, logf

    meta = jnp.broadcast_to(meta_tokens[None].astype(x_prompt.dtype), (x_prompt.shape[0], N_META, D_MODEL))
    h_p = jnp.concatenate([meta, x_prompt], axis=1)
    ctx0 = jnp.zeros((N_A_LAYERS, x_prompt.shape[0], CONV_WIDTH - 1, D_CONV), x_prompt.dtype)
    h_p, conv_state_prompt, k_prompt, v_prompt, logf_prompt = trunk(h_p, ctx0, prompt_attend)
    y_prompt = h_p[:, N_META:]

    n_b, n_pg = page_table.shape
    past = n_pg * PAGE_SIZE
    k_past = cache_k[page_table].reshape(n_b, past, N_HEADS, HEAD_DIM)
    v_past = cache_v[page_table].reshape(n_b, past, N_HEADS, HEAD_DIM)
    logf_past = cache_logf[page_table].reshape(n_b, past, N_HEADS).astype(jnp.float32)

    def attend_s(q, k, v, lf):
        return sample_attend(q, k, v, lf, k_past, v_past, logf_past)

    y_sample, conv_state_sample, k_sample, v_sample, logf_sample = trunk(x_sample, state_conv, attend_s)

    return (y_prompt, y_sample, conv_state_prompt, k_prompt, v_prompt, logf_prompt,
            conv_state_sample, k_sample, v_sample, logf_sample)
```

```python
import functools

import jax
import jax.numpy as jnp
from jax import lax
from jax.experimental import pallas as pl
from jax.experimental.pallas import tpu as pltpu

F32 = jnp.float32
BF = jnp.bfloat16

D = 1024
H = 16
DH = 64
DFF = 4096
CW = 31
NMETA = 16
PAGE = 128
EPS = 1e-6
NEG_INF = -1e30
CTX = 32
LANES = 128
ONES_BLK = 256
VMEM_LIMIT = 56 * 1024 * 1024


def _cparams(sem):
    return pltpu.CompilerParams(dimension_semantics=sem, vmem_limit_bytes=VMEM_LIMIT)


def _const_spec(shape):
    return pl.BlockSpec(shape, lambda *_: (0,) * len(shape), pipeline_mode=pl.Buffered(1))


def _dot(a, b):
    return jnp.dot(a, b, preferred_element_type=F32)


def _dot_nt(a, b):
    return lax.dot_general(a, b, (((1,), (1,)), ((), ())), preferred_element_type=F32)


def _rms_scale(x):
    return lax.rsqrt(jnp.mean(x * x, axis=-1, keepdims=True) + EPS)


def _dot3_l(x, m):
    a1 = x.astype(BF)
    r1 = x - a1.astype(F32)
    a2 = r1.astype(BF)
    a3 = (r1 - a2.astype(F32)).astype(BF)
    return _dot(a1, m) + _dot(a2, m) + _dot(a3, m)


def _col_from_row(row, n):
    eye = lax.broadcasted_iota(jnp.int32, (n, n), 0) == lax.broadcasted_iota(jnp.int32, (n, n), 1)
    return jnp.sum(jnp.where(eye, jnp.broadcast_to(row, (n, n)), 0.0), axis=1, keepdims=True)


def _conv_tail(y, bdw, lng, lnb):
    y = y + bdw
    mu = jnp.mean(y, axis=-1, keepdims=True)
    yc = y - mu
    yn = yc * lax.rsqrt(jnp.mean(yc * yc, axis=-1, keepdims=True) + EPS)
    yn = yn * lng + lnb
    return yn * jax.nn.sigmoid(yn)


def _glu_in(x, g, win_ref, bin_ref):
    hn = (x * _rms_scale(x) * g).astype(BF)
    a = _dot(hn, win_ref[:, :D]) + bin_ref[:, :D]
    gate = _dot(hn, win_ref[:, D:]) + bin_ref[:, D:]
    return a * jax.nn.sigmoid(gate)


def _conv_front_kernel(x_ref, g_ref, win_ref, bin_ref, wb_ref, bdw_ref, lng_ref, lnb_ref,
                       o_ref, tail_ref, s_ref, *, ts, ch):
    t = pl.program_id(1)

    @pl.when(t == 0)
    def _():
        s_ref[0:CTX, :] = jnp.zeros((CTX, D), F32)

    @pl.when(t > 0)
    def _():
        s_ref[0:CTX, :] = s_ref[ts:ts + CTX, :]

    s_ref[CTX:CTX + ts, :] = _glu_in(x_ref[0], g_ref[...], win_ref, bin_ref)
    tail_ref[0] = s_ref[ts:ts + CTX, :]

    bdw = bdw_ref[...]
    lng = lng_ref[...]
    lnb = lnb_ref[...]

    def body(c, carry):
        r0 = pl.multiple_of(c * ch, ch)
        win = s_ref[pl.ds(r0, ch + CTX), :]
        acc = jnp.zeros((ch // 8, 8, D), F32)
        for j in range(CW):
            off = CTX - (CW - 1) + j
            acc = acc + win[off:off + ch, :].reshape(ch // 8, 8, D) * wb_ref[8 * j:8 * j + 8, :][None]
        o_ref[0, pl.ds(r0, ch), :] = _conv_tail(acc.reshape(ch, D), bdw, lng, lnb).astype(BF)
        return carry

    lax.fori_loop(0, ts // ch, body, 0)


def _conv_front(h, g, win, b_in, wb, bdw, lng, lnb, *, ts=688, ch=16):
    b, l, _ = h.shape
    assert l % ts == 0 and ts % ch == 0
    row = lambda n: _const_spec((1, n))
    return pl.pallas_call(
        functools.partial(_conv_front_kernel, ts=ts, ch=ch),
        grid=(b, l // ts),
        in_specs=[
            pl.BlockSpec((1, ts, D), lambda i, t: (i, t, 0)),
            row(D), _const_spec((D, 2 * D)), row(2 * D), _const_spec((CW * 8, D)), row(D), row(D), row(D),
        ],
        out_specs=[
            pl.BlockSpec((1, ts, D), lambda i, t: (i, t, 0)),
            pl.BlockSpec((1, CTX, D), lambda i, t: (i, 0, 0)),
        ],
        out_shape=[
            jax.ShapeDtypeStruct((b, l, D), BF),
            jax.ShapeDtypeStruct((b, CTX, D), F32),
        ],
        scratch_shapes=[pltpu.VMEM((ts + CTX, D), F32)],
        compiler_params=_cparams(("arbitrary", "arbitrary")),
        name="conv_front",
    )(h, g, win, b_in, wb, bdw, lng, lnb)


def _conv_sample_kernel(x_ref, st_ref, g_ref, win_ref, bin_ref, wdw_ref, bdw_ref, lng_ref, lnb_ref,
                        o_ref, u_ref):
    u = _glu_in(x_ref[...], g_ref[...], win_ref, bin_ref)
    u_ref[...] = u
    y = u * wdw_ref[CW - 1:CW, :]
    for j in range(CW - 1):
        y = y + st_ref[j] * wdw_ref[j:j + 1, :]
    o_ref[...] = _conv_tail(y, bdw_ref[...], lng_ref[...], lnb_ref[...]).astype(BF)


def _conv_sample(x, state, g, win, b_in, wdw, bdw, lng, lnb):
    n = x.shape[0]
    return pl.pallas_call(
        _conv_sample_kernel,
        out_shape=[jax.ShapeDtypeStruct((n, D), BF), jax.ShapeDtypeStruct((n, D), F32)],
        compiler_params=pltpu.CompilerParams(vmem_limit_bytes=VMEM_LIMIT),
        name="conv_sample",
    )(x, state, g, win, b_in, wdw, bdw, lng, lnb)


def _proj_mlp_kernel(h_ref, o_ref, wp_ref, bp_ref, g_ref, wup_ref, wdn_ref, out_ref, *, fc):
    x = h_ref[...] + _dot(o_ref[...], wp_ref[...]) + bp_ref[...]
    hn = (x * _rms_scale(x) * g_ref[...]).astype(BF)
    acc = x
    for c in range(DFF // fc):
        hu = _dot(hn, wup_ref[:, c * fc:(c + 1) * fc])
        a = jnp.square(jnp.maximum(hu, 0.0)).astype(BF)
        acc = acc + _dot(a, wdn_ref[c * fc:(c + 1) * fc, :])
    out_ref[...] = acc


def _proj_mlp(h, o, wp, bp, g, wup, wdn, *, tm, fc=1024):
    t = h.shape[0]
    assert t % tm == 0
    tile = lambda: pl.BlockSpec((tm, D), lambda i: (i, 0))
    return pl.pallas_call(
        functools.partial(_proj_mlp_kernel, fc=fc),
        grid=(t // tm,),
        in_specs=[tile(), tile(), _const_spec((D, D)), _const_spec((1, D)), _const_spec((1, D)),
                  _const_spec((D, DFF)), _const_spec((DFF, D))],
        out_specs=tile(),
        out_shape=jax.ShapeDtypeStruct((t, D), F32),
        compiler_params=_cparams(("arbitrary",)),
        name="proj_mlp",
    )(h, o, wp, bp, g, wup, wdn)


def _head_rms(x, g, ones_ref):
    sq = (x * x).astype(BF)
    parts = [_dot(sq[:, i * ONES_BLK:(i + 1) * ONES_BLK], ones_ref[...]) for i in range(x.shape[1] // ONES_BLK)]
    ms = (parts[0] if len(parts) == 1 else jnp.concatenate(parts, axis=1)) * (1.0 / DH)
    return x * lax.rsqrt(ms + EPS) * g


def _log_sigmoid(x):
    return jnp.minimum(x, 0.0) - jnp.log1p(jnp.exp(-jnp.abs(x)))


def _kvq_rows_kernel(h_ref, kvg_ref, bg_ref, wkt_ref, wvt_ref, wft_ref, bf_ref, wq_ref, kg_ref, qg_ref, ones_ref,
                     k_ref, v_ref, lf_ref, q_ref):
    h = h_ref[...]
    hs = h * _rms_scale(h)
    hn_kv = (hs * kvg_ref[...]).astype(BF)
    hn_q = (hs * bg_ref[...]).astype(BF)
    k_ref[...] = _head_rms(_dot_nt(hn_kv, wkt_ref[...]), kg_ref[...], ones_ref)
    v_ref[...] = _dot_nt(hn_kv, wvt_ref[...])
    lf_ref[...] = _log_sigmoid(_dot_nt(hn_kv, wft_ref[...]) + bf_ref[...])
    q_ref[...] = _head_rms(_dot(hn_q, wq_ref[...]), qg_ref[...], ones_ref) * (DH ** -0.5)


def _kvq_rows(h, kvg, bg, wkt, wvt, wft, bf_row, wq, kg, qg, ones_blk):
    n = h.shape[0]
    return pl.pallas_call(
        _kvq_rows_kernel,
        out_shape=[jax.ShapeDtypeStruct((n, D), F32), jax.ShapeDtypeStruct((n, D), F32),
                   jax.ShapeDtypeStruct((n, H), F32), jax.ShapeDtypeStruct((n, D), F32)],
        compiler_params=pltpu.CompilerParams(vmem_limit_bytes=VMEM_LIMIT),
        name="kvq_rows",
    )(h, kvg, bg, wkt, wvt, wft, bf_row, wq, kg, qg, ones_blk)


def _kvq_seq_kernel(h_ref, kvg_ref, bg_ref, wkt_ref, wvt_ref, wft_ref, bfc_ref, wq_ref, kgc_ref, qg_ref, ones_ref,
                    kt_ref, vt_ref, lft_ref, q_ref, hkv_ref, hq_ref, *, hg, rc):
    j = pl.program_id(1)
    l = hkv_ref.shape[0]

    @pl.when(j == 0)
    def _():
        def body(c, carry):
            r0 = pl.multiple_of(c * rc, rc)
            h = h_ref[0, pl.ds(r0, rc), :]
            hs = h * _rms_scale(h)
            hkv_ref[pl.ds(r0, rc), :] = (hs * kvg_ref[...]).astype(BF)
            hq_ref[pl.ds(r0, rc), :] = (hs * bg_ref[...]).astype(BF)
            return carry

        lax.fori_loop(0, l // rc, body, 0)
        lft_ref[0] = _log_sigmoid(_dot_nt(wft_ref[...], hkv_ref[...]) + bfc_ref[...])

    hkv = hkv_ref[...]
    kt = _dot_nt(wkt_ref[...], hkv).reshape(hg, DH, l)
    ms = jnp.mean(kt * kt, axis=1, keepdims=True)
    kt_ref[0] = kt * lax.rsqrt(ms + EPS) * kgc_ref[...][None]
    vt_ref[0] = _dot_nt(wvt_ref[...], hkv).reshape(hg, DH, l)
    q = _head_rms(_dot(hq_ref[...], wq_ref[...]), qg_ref[...], ones_ref)
    q_ref[0] = (q * (DH ** -0.5)).astype(BF)


def _kvq_seq(h, kvg, bg, wkt, wvt, wft, bf_col, wq, kg_col, qg, ones_blk, *, hg=4, rc=688):
    b, l, _ = h.shape
    w = hg * DH
    assert w == ONES_BLK and l % rc == 0 and rc % 16 == 0
    row = lambda: _const_spec((1, D))
    return pl.pallas_call(
        functools.partial(_kvq_seq_kernel, hg=hg, rc=rc),
        grid=(b, H // hg),
        in_specs=[
            pl.BlockSpec((1, l, D), lambda i, j: (i, 0, 0), pipeline_mode=pl.Buffered(1)),
            row(), row(),
            pl.BlockSpec((w, D), lambda i, j: (j, 0)),
            pl.BlockSpec((w, D), lambda i, j: (j, 0)),
            _const_spec((H, D)), _const_spec((H, 1)),
            pl.BlockSpec((D, w), lambda i, j: (0, j)),
            _const_spec((DH, 1)), _const_spec((1, w)), _const_spec((ONES_BLK, ONES_BLK)),
        ],
        out_specs=[
            pl.BlockSpec((1, hg, DH, l), lambda i, j: (i, j, 0, 0)),
            pl.BlockSpec((1, hg, DH, l), lambda i, j: (i, j, 0, 0)),
            pl.BlockSpec((1, H, l), lambda i, j: (i, 0, 0)),
            pl.BlockSpec((1, l, w), lambda i, j: (i, 0, j)),
        ],
        out_shape=[
            jax.ShapeDtypeStruct((b, H, DH, l), F32), jax.ShapeDtypeStruct((b, H, DH, l), F32),
            jax.ShapeDtypeStruct((b, H, l), F32), jax.ShapeDtypeStruct((b, l, D), BF),
        ],
        scratch_shapes=[pltpu.VMEM((l, D), BF), pltpu.VMEM((l, D), BF)],
        compiler_params=_cparams(("arbitrary", "arbitrary")),
        name="kvq_seq",
    )(h, kvg, bg, wkt, wvt, wft, bf_col, wq, kg_col, qg, ones_blk)


def _cumsum_kernel(x_ref, c_ref, *, nblk):
    r = lax.broadcasted_iota(jnp.int32, (LANES, LANES), 0)
    c = lax.broadcasted_iota(jnp.int32, (LANES, LANES), 1)
    upper = (r <= c).astype(BF)
    carry = jnp.zeros((H, 1), F32)
    for b in range(nblk):
        cb = _dot3_l(x_ref[0, :, b * LANES:(b + 1) * LANES], upper) + carry
        c_ref[0, :, b * LANES:(b + 1) * LANES] = cb
        carry = cb[:, LANES - 1:LANES]


def _cumsum_seq(xt):
    b, h, lp = xt.shape
    return pl.pallas_call(
        functools.partial(_cumsum_kernel, nblk=lp // LANES),
        grid=(b,),
        in_specs=[pl.BlockSpec((1, h, lp), lambda i: (i, 0, 0))],
        out_specs=pl.BlockSpec((1, h, lp), lambda i: (i, 0, 0)),
        out_shape=jax.ShapeDtypeStruct((b, h, lp), F32),
        compiler_params=_cparams(("arbitrary",)),
        name="cumsum_seq",
    )(xt)


def _attn_prompt_kernel(q_ref, kt_ref, vt_ref, c_ref, o_ref, ktb_ref, vtb_ref, *, l):
    blk = LANES
    nblk = c_ref.shape[3]
    lp = nblk * blk
    nfull, tail = l // blk, l % blk
    if tail:
        ktb_ref[:, :, nfull * blk:lp] = jnp.zeros((2, DH, lp - nfull * blk), BF)
        vtb_ref[:, :, nfull * blk:lp] = jnp.zeros((2, DH, lp - nfull * blk), BF)
    ktb_ref[:, :, 0:l] = kt_ref[0].astype(BF)
    vtb_ref[:, :, 0:l] = vt_ref[0].astype(BF)

    causal = (lax.broadcasted_iota(jnp.int32, (blk, blk), 1) <= lax.broadcasted_iota(jnp.int32, (blk, blk), 0))

    def q_block(qi, nr):
        q0 = pl.multiple_of(qi * blk, blk)
        q2 = q_ref[0, pl.ds(q0, nr), :]
        qe = [q2[:, e * DH:(e + 1) * DH] for e in range(2)]
        cq = [_col_from_row(c_ref[0, 0, e, pl.ds(qi, 1), :], blk)[:nr] for e in range(2)]

        def chunk(kj, state, masked):
            k0 = pl.multiple_of(kj * blk, blk)
            out = []
            for e in range(2):
                m, lsum, acc = state[e]
                s = _dot(qe[e], ktb_ref[e, :, pl.ds(k0, blk)])
                s = s + (cq[e] - c_ref[0, 0, e, pl.ds(kj, 1), :])
                if masked:
                    s = jnp.where(causal[:nr], s, NEG_INF)
                m_new = jnp.maximum(m, jnp.max(s, axis=1, keepdims=True))
                alpha = jnp.exp(m - m_new)
                p = jnp.exp(s - m_new)
                lsum = alpha * lsum + jnp.sum(p, axis=1, keepdims=True)
                acc = acc * alpha + _dot_nt(p.astype(BF), vtb_ref[e, :, pl.ds(k0, blk)])
                out.append((m_new, lsum, acc))
            return tuple(out)

        init = tuple((jnp.full((nr, 1), NEG_INF, F32), jnp.zeros((nr, 1), F32), jnp.zeros((nr, DH), F32))
                     for _ in range(2))
        state = lax.fori_loop(0, qi, lambda kj, st: chunk(kj, st, False), init)
        state = chunk(qi, state, True)
        o = jnp.concatenate([state[e][2] / state[e][1] for e in range(2)], axis=1)
        o_ref[0, pl.ds(q0, nr), :] = o.astype(BF)

    def outer(qi, carry):
        q_block(qi, blk)
        return carry

    lax.fori_loop(0, nfull, outer, 0)
    if tail:
        q_block(nfull, tail)


def _attn_prompt(q, kt, vt, c):
    b, l, _ = q.shape
    nblk = c.shape[3]
    lp = nblk * LANES
    return pl.pallas_call(
        functools.partial(_attn_prompt_kernel, l=l),
        grid=(b, H // 2),
        in_specs=[pl.BlockSpec((1, l, LANES), lambda i, j: (i, 0, j)),
                  pl.BlockSpec((1, 2, DH, l), lambda i, j: (i, j, 0, 0)),
                  pl.BlockSpec((1, 2, DH, l), lambda i, j: (i, j, 0, 0)),
                  pl.BlockSpec((1, 1, 2, nblk, LANES), lambda i, j: (i, j, 0, 0, 0))],
        out_specs=pl.BlockSpec((1, l, LANES), lambda i, j: (i, 0, j)),
        out_shape=jax.ShapeDtypeStruct((b, l, D), BF),
        scratch_shapes=[pltpu.VMEM((2, DH, lp), BF), pltpu.VMEM((2, DH, lp), BF)],
        compiler_params=_cparams(("arbitrary", "arbitrary")),
        name="attn_prompt",
    )(q, kt, vt, c)


def _attn_sample_kernel(pt_ref, q_ref, kn_ref, vn_ref, lfn_ref, *refs, pp):
    k_refs = refs[:pp]
    v_refs = refs[pp:2 * pp]
    lf_refs = refs[2 * pp:3 * pp]
    o_ref = refs[3 * pp]
    m_ref, l_ref, carry_ref, qb_ref, acc_ref = refs[3 * pp + 1:]
    g = pl.program_id(1)

    @pl.when(g == 0)
    def _():
        q = q_ref[0]
        lane0 = lax.broadcasted_iota(jnp.int32, (DH, PAGE), 1) == 0
        for h in range(H):
            qb_ref[h] = jnp.broadcast_to(_col_from_row(q[h:h + 1, :], DH), (DH, PAGE))
            vcol = _col_from_row(vn_ref[0, h:h + 1, :], DH)
            acc_ref[h] = jnp.where(lane0, jnp.broadcast_to(vcol, (DH, PAGE)), 0.0)
        m_ref[...] = jnp.sum(q * kn_ref[0], axis=1, keepdims=True)
        l_ref[...] = jnp.ones((H, 1), F32)
        carry_ref[...] = lfn_ref[0]

    r = lax.broadcasted_iota(jnp.int32, (PAGE, PAGE), 0)
    c = lax.broadcasted_iota(jnp.int32, (PAGE, PAGE), 1)
    later = (r > c).astype(BF)

    lf_all = jnp.concatenate([lf_refs[i][0] for i in range(pp)], axis=0)
    within = _dot3_l(lf_all, later)
    carry = carry_ref[...]
    qb = qb_ref[...]
    scores = []
    for i in range(pp):
        w_i = within[i * H:(i + 1) * H]
        scores.append(jnp.sum(k_refs[i][0] * qb, axis=1) + (w_i + carry))
        carry = carry + (w_i[:, 0:1] + lf_refs[i][0][:, 0:1])
    carry_ref[...] = carry

    m_old = m_ref[...]
    m_new = m_old
    for s in scores:
        m_new = jnp.maximum(m_new, jnp.max(s, axis=1, keepdims=True))
    alpha = jnp.exp(m_old - m_new)
    psum = jnp.zeros((H, 1), F32)
    acc = acc_ref[...] * jnp.broadcast_to(alpha, (H, PAGE))[:, None, :]
    for i in range(pp):
        p = jnp.exp(scores[i] - m_new)
        psum = psum + jnp.sum(p, axis=1, keepdims=True)
        acc = acc + p[:, None, :] * v_refs[i][0]
    acc_ref[...] = acc
    l_ref[...] = l_ref[...] * alpha + psum
    m_ref[...] = m_new

    @pl.when(g == pl.num_programs(1) - 1)
    def _():
        o_ref[0] = (jnp.sum(acc_ref[...], axis=2) / l_ref[...]).astype(o_ref.dtype)


def _attn_sample(page_table, q, k_new, v_new, lf_new, cache_kt, cache_vt, cache_lft, *, pp=8):
    n, npages = page_table.shape
    assert npages % pp == 0
    last = npages - 1

    def kv_spec(i):
        return pl.BlockSpec((1, H, DH, PAGE), lambda b, g, pt: (pt[b, last - (g * pp + i)], 0, 0, 0))

    def lf_spec(i):
        return pl.BlockSpec((1, H, PAGE), lambda b, g, pt: (pt[b, last - (g * pp + i)], 0, 0))

    per_seq = lambda w: pl.BlockSpec((1, H, w), lambda b, g, pt: (b, 0, 0))
    grid_spec = pltpu.PrefetchScalarGridSpec(
        num_scalar_prefetch=1,
        grid=(n, npages // pp),
        in_specs=[per_seq(DH), per_seq(DH), per_seq(DH), per_seq(1)]
        + [kv_spec(i) for i in range(pp)] + [kv_spec(i) for i in range(pp)] + [lf_spec(i) for i in range(pp)],
        out_specs=per_seq(DH),
        scratch_shapes=[pltpu.VMEM((H, 1), F32), pltpu.VMEM((H, 1), F32), pltpu.VMEM((H, 1), F32),
                        pltpu.VMEM((H, DH, PAGE), F32), pltpu.VMEM((H, DH, PAGE), F32)],
    )
    return pl.pallas_call(
        functools.partial(_attn_sample_kernel, pp=pp),
        grid_spec=grid_spec,
        out_shape=jax.ShapeDtypeStruct((n, H, DH), BF),
        compiler_params=_cparams(("arbitrary", "arbitrary")),
        name="attn_sample",
    )(page_table, q, k_new, v_new, lf_new, *([cache_kt] * pp), *([cache_vt] * pp), *([cache_lft] * pp))


def kernel(x_prompt, x_sample, state_conv, cache_k, cache_v, cache_logf, page_table, meta_tokens, a_norm_g, a_w_in, a_b_in, a_w_dw, a_b_dw, a_ln_g, a_ln_b, a_w_out, a_b_out, kv_norm_g, w_kvf, b_f, k_norm_g, b_norm_g, b_w_q, q_norm_g, b_w_o, mlp_norm_g, mlp_w_up, mlp_w_down):
    nb, seq, _ = x_prompt.shape
    ns = x_sample.shape[0]
    l = NMETA + seq
    t = nb * l
    row = lambda x: x.reshape(1, -1).astype(F32)

    w_in = a_w_in[0].astype(BF)
    b_in = row(a_b_in[0])
    w_dw = a_w_dw[0]
    w_dw_rep = jnp.repeat(w_dw, 8, axis=0)
    w_out = a_w_out[0].astype(BF)
    w_kvf_t = w_kvf.T.astype(BF)
    w_kt, w_vt, w_ft = w_kvf_t[:D], w_kvf_t[D:2 * D], w_kvf_t[2 * D:]
    w_q = b_w_q[0].astype(BF)
    w_o = b_w_o[0].astype(BF)
    w_up = mlp_w_up.astype(BF)
    w_dn = mlp_w_down.astype(BF)
    grp = jnp.arange(ONES_BLK) // DH
    ones_blk = (grp[:, None] == grp[None, :]).astype(BF)
    zero_b = jnp.zeros((1, D), F32)

    conv_w = (row(a_norm_g[0]), w_in, b_in)
    conv_post = (row(a_b_dw[0]), row(a_ln_g[0]), row(a_ln_b[0]))
    mlp0 = (w_out, row(a_b_out[0]), row(mlp_norm_g[0]), w_up[0], w_dn[0])
    mlp1 = (w_o, zero_b, row(mlp_norm_g[1]), w_up[1], w_dn[1])
    norms = (row(kv_norm_g), row(b_norm_g[0]))

    meta = jnp.broadcast_to(meta_tokens[None].astype(x_prompt.dtype), (nb, NMETA, D))
    h0 = jnp.concatenate([meta, x_prompt], axis=1)
    o0, tail = _conv_front(h0, *conv_w, w_dw_rep, *conv_post)
    conv_state_prompt = tail[None, :, CTX - (CW - 1):, :]
    h1 = _proj_mlp(h0.reshape(t, D), o0.reshape(t, D), *mlp0, tm=688)
    kt_p, vt_p, lft_p, q_p = _kvq_seq(
        h1.reshape(nb, l, D), *norms, w_kt, w_vt, w_ft, b_f.reshape(H, 1), w_q,
        k_norm_g.reshape(DH, 1), row(jnp.tile(q_norm_g[0], ONES_BLK // DH)), ones_blk)
    lp = -(-l // LANES) * LANES
    c_t = _cumsum_seq(jnp.pad(lft_p, ((0, 0), (0, 0), (0, lp - l))))
    o1 = _attn_prompt(q_p, kt_p, vt_p, c_t.reshape(nb, H // 2, 2, lp // LANES, LANES))
    h2 = _proj_mlp(h1, o1.reshape(t, D), *mlp1, tm=688)
    y_prompt = h2.reshape(nb, l, D)[:, NMETA:]

    xs = x_sample.reshape(ns, D)
    st = jnp.transpose(state_conv[0], (1, 0, 2))
    o0s, u_s = _conv_sample(xs, st, *conv_w, w_dw, *conv_post)
    conv_state_sample = jnp.transpose(jnp.concatenate([st[1:], u_s[None]], axis=0), (1, 0, 2))[None]
    h1s = _proj_mlp(xs, o0s, *mlp0, tm=ns)
    k_s, v_s, lf_s, q_s = _kvq_rows(h1s, *norms, w_kt, w_vt, w_ft, row(b_f), w_q,
                                    row(jnp.tile(k_norm_g, H)), row(jnp.tile(q_norm_g[0], H)), ones_blk)
    o1s = _attn_sample(page_table, q_s.reshape(ns, H, DH), k_s.reshape(ns, H, DH), v_s.reshape(ns, H, DH),
                       lf_s.reshape(ns, H, 1),
                       jnp.transpose(cache_k, (0, 2, 3, 1)), jnp.transpose(cache_v, (0, 2, 3, 1)),
                       jnp.transpose(cache_logf, (0, 2, 1)))
    h2s = _proj_mlp(h1s, o1s.reshape(ns, D), *mlp1, tm=ns)

    return (y_prompt, h2s.reshape(ns, 1, D), conv_state_prompt,
            jnp.transpose(kt_p, (0, 3, 1, 2)), jnp.transpose(vt_p, (0, 3, 1, 2)), jnp.transpose(lft_p, (0, 2, 1)),
            conv_state_sample, k_s.reshape(ns, 1, H, DH), v_s.reshape(ns, 1, H, DH), lf_s.reshape(ns, 1, H))
```

```python
import functools

import jax
import jax.numpy as jnp
from jax import lax
from jax.experimental import pallas as pl
from jax.experimental.pallas import tpu as pltpu

F32 = jnp.float32
BF = jnp.bfloat16

D = 1024
H = 16
DH = 64
DFF = 4096
CW = 31
NMETA = 16
PAGE = 128
EPS = 1e-6
NEG_INF = -1e30
CTX = 32
LANES = 128
ONES_BLK = 256
AUG = 16
LOG2E = 1.4426950408889634
VMEM_LIMIT = 56 * 1024 * 1024


def _cparams(sem):
    return pltpu.CompilerParams(dimension_semantics=sem, vmem_limit_bytes=VMEM_LIMIT)


def _const_spec(shape):
    return pl.BlockSpec(shape, lambda *_: (0,) * len(shape), pipeline_mode=pl.Buffered(1))


def _dot(a, b):
    return jnp.dot(a, b, preferred_element_type=F32)


def _dot_nt(a, b):
    return lax.dot_general(a, b, (((1,), (1,)), ((), ())), preferred_element_type=F32)


def _rms_scale(x):
    return lax.rsqrt(jnp.mean(x * x, axis=-1, keepdims=True) + EPS)


def _split3(x):
    a1 = x.astype(BF)
    r1 = x - a1.astype(F32)
    a2 = r1.astype(BF)
    return a1, a2, (r1 - a2.astype(F32)).astype(BF)


def _dot3_l(x, m):
    a1, a2, a3 = _split3(x)
    return _dot(a1, m) + _dot(a2, m) + _dot(a3, m)


def _col_from_row(row, n):
    eye = lax.broadcasted_iota(jnp.int32, (n, n), 0) == lax.broadcasted_iota(jnp.int32, (n, n), 1)
    return jnp.sum(jnp.where(eye, jnp.broadcast_to(row, (n, n)), 0.0), axis=1, keepdims=True)


def _conv_tail(y, bdw, lng, lnb):
    y = y + bdw
    mu = jnp.mean(y, axis=-1, keepdims=True)
    yc = y - mu
    yn = yc * lax.rsqrt(jnp.mean(yc * yc, axis=-1, keepdims=True) + EPS)
    yn = yn * lng + lnb
    return yn * jax.nn.sigmoid(yn)


def _glu_in(x, g, win_ref, bin_ref):
    hn = (x * _rms_scale(x) * g).astype(BF)
    a = _dot(hn, win_ref[:, :D]) + bin_ref[:, :D]
    gate = _dot(hn, win_ref[:, D:]) + bin_ref[:, D:]
    return a * jax.nn.sigmoid(gate)


def _conv_front_kernel(x_ref, g_ref, win_ref, bin_ref, wb_ref, bdw_ref, lng_ref, lnb_ref,
                       o_ref, tail_ref, s_ref, sh_ref, *, ts, ch):
    t = pl.program_id(1)
    n_sh = sh_ref.shape[1]

    @pl.when(t == 0)
    def _():
        s_ref[0:CTX, :] = jnp.zeros((CTX, D), F32)

    @pl.when(t > 0)
    def _():
        s_ref[0:CTX, :] = s_ref[ts:ts + CTX, :]

    s_ref[CTX:CTX + ts, :] = _glu_in(x_ref[0], g_ref[...], win_ref, bin_ref)
    tail_ref[0] = s_ref[ts:ts + CTX, :]
    for r in range(1, 8):
        sh_ref[r - 1] = s_ref[r:r + n_sh, :]

    bdw = bdw_ref[...]
    lng = lng_ref[...]
    lnb = lnb_ref[...]

    def aligned(x, m):
        return x if isinstance(x, int) else pl.multiple_of(x, m)

    def taps(c):
        acc = jnp.zeros((ch // 8, 8, D), F32)
        for j in range(CW):
            off = CTX - (CW - 1) + j
            src = s_ref if off % 8 == 0 else sh_ref.at[off % 8 - 1]
            win = src[pl.ds(aligned(c * ch + off // 8 * 8, 8), ch), :]
            acc = acc + win.reshape(ch // 8, 8, D) * wb_ref[8 * j:8 * j + 8, :][None]
        return acc.reshape(ch, D)

    def finish(c, y):
        o_ref[0, pl.ds(aligned(c * ch, ch), ch), :] = _conv_tail(y, bdw, lng, lnb).astype(BF)

    def body(c, y_prev):
        y = taps(c)
        finish(c - 1, y_prev)
        return y

    n_ch = ts // ch
    finish(n_ch - 1, lax.fori_loop(1, n_ch, body, taps(0)))


def _conv_front(h, g, win, b_in, wb, bdw, lng, lnb, *, ts=688, ch=16):
    b, l, _ = h.shape
    assert l % ts == 0 and ts % ch == 0 and ch % 8 == 0
    row = lambda n: _const_spec((1, n))
    return pl.pallas_call(
        functools.partial(_conv_front_kernel, ts=ts, ch=ch),
        grid=(b, l // ts),
        in_specs=[
            pl.BlockSpec((1, ts, D), lambda i, t: (i, t, 0)),
            row(D), _const_spec((D, 2 * D)), row(2 * D), _const_spec((CW * 8, D)), row(D), row(D), row(D),
        ],
        out_specs=[
            pl.BlockSpec((1, ts, D), lambda i, t: (i, t, 0)),
            pl.BlockSpec((1, CTX, D), lambda i, t: (i, 0, 0)),
        ],
        out_shape=[
            jax.ShapeDtypeStruct((b, l, D), BF),
            jax.ShapeDtypeStruct((b, CTX, D), F32),
        ],
        scratch_shapes=[pltpu.VMEM((ts + CTX, D), F32), pltpu.VMEM((7, ts + CTX - 8, D), F32)],
        compiler_params=_cparams(("arbitrary", "arbitrary")),
        name="conv_front",
    )(h, g, win, b_in, wb, bdw, lng, lnb)


def _conv_sample_kernel(x_ref, st_ref, g_ref, win_ref, bin_ref, wdw_ref, bdw_ref, lng_ref, lnb_ref,
                        o_ref, u_ref):
    u = _glu_in(x_ref[...], g_ref[...], win_ref, bin_ref)
    u_ref[...] = u
    y = u * wdw_ref[CW - 1:CW, :]
    for j in range(CW - 1):
        y = y + st_ref[j] * wdw_ref[j:j + 1, :]
    o_ref[...] = _conv_tail(y, bdw_ref[...], lng_ref[...], lnb_ref[...]).astype(BF)


def _conv_sample(x, state, g, win, b_in, wdw, bdw, lng, lnb):
    n = x.shape[0]
    return pl.pallas_call(
        _conv_sample_kernel,
        out_shape=[jax.ShapeDtypeStruct((n, D), BF), jax.ShapeDtypeStruct((n, D), F32)],
        compiler_params=pltpu.CompilerParams(vmem_limit_bytes=VMEM_LIMIT),
        name="conv_sample",
    )(x, state, g, win, b_in, wdw, bdw, lng, lnb)


def _proj_mlp_kernel(h_ref, o_ref, wp_ref, bp_ref, g_ref, wup_ref, wdn_ref, out_ref, *, fc):
    x = h_ref[...] + _dot(o_ref[...], wp_ref[...]) + bp_ref[...]
    hn = (x * _rms_scale(x) * g_ref[...]).astype(BF)
    acc = x
    for c in range(DFF // fc):
        hu = _dot(hn, wup_ref[:, c * fc:(c + 1) * fc])
        a = jnp.square(jnp.maximum(hu, 0.0)).astype(BF)
        acc = acc + _dot(a, wdn_ref[c * fc:(c + 1) * fc, :])
    out_ref[...] = acc


def _proj_mlp(h, o, wp, bp, g, wup, wdn, *, tm, fc=1024):
    t = h.shape[0]
    assert t % tm == 0
    tile = lambda: pl.BlockSpec((tm, D), lambda i: (i, 0))
    return pl.pallas_call(
        functools.partial(_proj_mlp_kernel, fc=fc),
        grid=(t // tm,),
        in_specs=[tile(), tile(), _const_spec((D, D)), _const_spec((1, D)), _const_spec((1, D)),
                  _const_spec((D, DFF)), _const_spec((DFF, D))],
        out_specs=tile(),
        out_shape=jax.ShapeDtypeStruct((t, D), F32),
        compiler_params=_cparams(("arbitrary",)),
        name="proj_mlp",
    )(h, o, wp, bp, g, wup, wdn)


def _head_rms(x, g, ones_ref):
    sq = (x * x).astype(BF)
    parts = [_dot(sq[:, i * ONES_BLK:(i + 1) * ONES_BLK], ones_ref[...]) for i in range(x.shape[1] // ONES_BLK)]
    ms = (parts[0] if len(parts) == 1 else jnp.concatenate(parts, axis=1)) * (1.0 / DH)
    return x * lax.rsqrt(ms + EPS) * g


def _log_sigmoid(x):
    return jnp.minimum(x, 0.0) - jnp.log1p(jnp.exp(-jnp.abs(x)))


def _kvq_rows_kernel(h_ref, kvg_ref, bg_ref, wkt_ref, wvt_ref, wft_ref, bf_ref, wq_ref, kg_ref, qg_ref, ones_ref,
                     k_ref, v_ref, lf_ref, q_ref):
    h = h_ref[...]
    hs = h * _rms_scale(h)
    hn_kv = (hs * kvg_ref[...]).astype(BF)
    hn_q = (hs * bg_ref[...]).astype(BF)
    k_ref[...] = _head_rms(_dot_nt(hn_kv, wkt_ref[...]), kg_ref[...], ones_ref)
    v_ref[...] = _dot_nt(hn_kv, wvt_ref[...])
    lf_ref[...] = _log_sigmoid(_dot_nt(hn_kv, wft_ref[...]) + bf_ref[...])
    q_ref[...] = _head_rms(_dot(hn_q, wq_ref[...]), qg_ref[...], ones_ref) * (DH ** -0.5)


def _kvq_rows(h, kvg, bg, wkt, wvt, wft, bf_row, wq, kg, qg, ones_blk):
    n = h.shape[0]
    return pl.pallas_call(
        _kvq_rows_kernel,
        out_shape=[jax.ShapeDtypeStruct((n, D), F32), jax.ShapeDtypeStruct((n, D), F32),
                   jax.ShapeDtypeStruct((n, H), F32), jax.ShapeDtypeStruct((n, D), F32)],
        compiler_params=pltpu.CompilerParams(vmem_limit_bytes=VMEM_LIMIT),
        name="kvq_rows",
    )(h, kvg, bg, wkt, wvt, wft, bf_row, wq, kg, qg, ones_blk)


def _head_rms_t(xt, g_col):
    return xt * lax.rsqrt(jnp.mean(xt * xt, axis=1, keepdims=True) + EPS) * g_col[None]


def _kvq_seq_kernel(h_ref, kvg_ref, bg_ref, wkt_ref, wvt_ref, wft_ref, bfc_ref, wqt_ref, kgc_ref, qgc_ref,
                    kt_ref, vt_ref, lft_ref, qt_ref, hkv_ref, hq_ref, *, hg, rc):
    j = pl.program_id(1)
    l = hkv_ref.shape[0]

    @pl.when(j == 0)
    def _():
        def body(c, carry):
            r0 = pl.multiple_of(c * rc, rc)
            h = h_ref[0, pl.ds(r0, rc), :]
            hs = h * _rms_scale(h)
            hkv_ref[pl.ds(r0, rc), :] = (hs * kvg_ref[...]).astype(BF)
            hq_ref[pl.ds(r0, rc), :] = (hs * bg_ref[...]).astype(BF)
            return carry

        lax.fori_loop(0, l // rc, body, 0)
        lft_ref[0] = _log_sigmoid(_dot_nt(wft_ref[...], hkv_ref[...]) + bfc_ref[...])

    hkv = hkv_ref[...]
    kt_ref[0] = _head_rms_t(_dot_nt(wkt_ref[...], hkv).reshape(hg, DH, l), kgc_ref[...])
    vt_ref[0] = _dot_nt(wvt_ref[...], hkv).reshape(hg, DH, l)
    qt = _head_rms_t(_dot_nt(wqt_ref[...], hq_ref[...]).reshape(hg, DH, l), qgc_ref[...])
    qt_ref[0] = (qt * (DH ** -0.5 * LOG2E)).astype(BF)


def _kvq_seq(h, kvg, bg, wkt, wvt, wft, bf_col, wqt, kg_col, qg_col, *, hg=4, rc=688):
    b, l, _ = h.shape
    w = hg * DH
    assert l % rc == 0 and rc % 16 == 0
    row = lambda: _const_spec((1, D))
    heads = lambda: pl.BlockSpec((1, hg, DH, l), lambda i, j: (i, j, 0, 0))
    wrows = lambda: pl.BlockSpec((w, D), lambda i, j: (j, 0))
    return pl.pallas_call(
        functools.partial(_kvq_seq_kernel, hg=hg, rc=rc),
        grid=(b, H // hg),
        in_specs=[
            pl.BlockSpec((1, l, D), lambda i, j: (i, 0, 0), pipeline_mode=pl.Buffered(1)),
            row(), row(), wrows(), wrows(), _const_spec((H, D)), _const_spec((H, 1)), wrows(),
            _const_spec((DH, 1)), _const_spec((DH, 1)),
        ],
        out_specs=[heads(), heads(), pl.BlockSpec((1, H, l), lambda i, j: (i, 0, 0)), heads()],
        out_shape=[
            jax.ShapeDtypeStruct((b, H, DH, l), F32), jax.ShapeDtypeStruct((b, H, DH, l), F32),
            jax.ShapeDtypeStruct((b, H, l), F32), jax.ShapeDtypeStruct((b, H, DH, l), BF),
        ],
        scratch_shapes=[pltpu.VMEM((l, D), BF), pltpu.VMEM((l, D), BF)],
        compiler_params=_cparams(("arbitrary", "arbitrary")),
        name="kvq_seq",
    )(h, kvg, bg, wkt, wvt, wft, bf_col, wqt, kg_col, qg_col)


def _cumsum_kernel(x_ref, c_ref, *, nblk):
    r = lax.broadcasted_iota(jnp.int32, (LANES, LANES), 0)
    c = lax.broadcasted_iota(jnp.int32, (LANES, LANES), 1)
    upper = (r <= c).astype(BF)
    carry = jnp.zeros((H, 1), F32)
    for b in range(nblk):
        cb = _dot3_l(x_ref[0, :, b * LANES:(b + 1) * LANES], upper) + carry
        c_ref[0, :, b * LANES:(b + 1) * LANES] = cb
        carry = cb[:, LANES - 1:LANES]


def _cumsum_seq(xt):
    b, h, lp = xt.shape
    return pl.pallas_call(
        functools.partial(_cumsum_kernel, nblk=lp // LANES),
        grid=(b,),
        in_specs=[pl.BlockSpec((1, h, lp), lambda i: (i, 0, 0))],
        out_specs=pl.BlockSpec((1, h, lp), lambda i: (i, 0, 0)),
        out_shape=jax.ShapeDtypeStruct((b, h, lp), F32),
        compiler_params=_cparams(("arbitrary",)),
        name="cumsum_seq",
    )(xt)


def _split3_f32(x):
    return [p.astype(F32) for p in _split3(x)]


def _decay_slots(idx, base, pieces):
    out = jnp.where((idx >= base) & (idx < base + 3), 1.0, 0.0)
    for i, p in enumerate(pieces):
        out = jnp.where(idx == base + 3 + i, p, out)
    return out.astype(BF)


def _attn_prompt_kernel(qt_ref, kt_ref, vt_ref, cf_ref, o_ref, ka_ref, kat_ref, vtb_ref, qtl_ref, tr_ref, s_ref,
                        *, l, qb):
    lp = ka_ref.shape[0]
    nfull, tail = l // qb, l % qb
    kw = 2 * DH
    cf = cf_ref[0, 0] * LOG2E

    tr_ref[:, l:lp] = jnp.zeros((kw, lp - l), F32)
    tr_ref[:, 0:l] = kt_ref[0].reshape(kw, l)
    ri = lax.broadcasted_iota(jnp.int32, (AUG, lp), 0)
    kaug = jnp.zeros((AUG, lp), F32)
    for e in range(2):
        kaug = jnp.where((ri >= 6 * e + 3) & (ri < 6 * e + 6), 1.0, kaug)
        for i, p in enumerate(_split3_f32(cf[e:e + 1, :])):
            kaug = jnp.where(ri == 6 * e + i, -p, kaug)
    kat_ref[0:kw, :] = tr_ref[...].astype(BF)
    kat_ref[kw:kw + AUG, :] = kaug.astype(BF)
    ka_ref[:, 0:kw] = tr_ref[...].T.astype(BF)
    kaug_pad = jnp.concatenate([kaug, jnp.zeros((kw - AUG, lp), F32)], axis=0)
    ka_ref[:, kw:kw + AUG] = kaug_pad.T[:, 0:AUG].astype(BF)
    vtb_ref[:, 0:DH, l:lp] = jnp.zeros((2, DH, lp - l), BF)
    vtb_ref[:, 0:DH, 0:l] = vt_ref[0].astype(BF)
    vtb_ref[:, DH:DH + AUG, :] = jnp.broadcast_to(
        jnp.where(lax.broadcasted_iota(jnp.int32, (AUG, lp), 0) == 0, 1.0, 0.0).astype(BF)[None], (2, AUG, lp))

    rq = lax.broadcasted_iota(jnp.int32, (AUG, qb), 0)
    zq = jnp.zeros((DH, qb), BF)
    tri = lax.broadcasted_iota(jnp.int32, (qb, qb), 0) <= lax.broadcasted_iota(jnp.int32, (qb, qb), 1)

    def q_aug(i, e):
        q0 = i * qb
        qt = qt_ref[0, e, :, q0:q0 + qb]
        slots = _decay_slots(rq, 6 * e, _split3_f32(cf[e:e + 1, q0:q0 + qb]))
        return jnp.concatenate([qt if e == 0 else zq, qt if e == 1 else zq, slots], axis=0)

    items = [(i, e) for i in range(nfull) for e in range(2)]
    state = {}

    def pass_a(n, c):
        i, e = items[n]
        if c == 0:
            state[n] = dict(qa=q_aug(i, e), m=None, acc=None)
        s = _dot(ka_ref[c * qb:(c + 1) * qb, :], state[n]["qa"])
        if c == i:
            s = jnp.where(tri, s, NEG_INF)
        s_ref[n % 2, c] = s
        mc = jnp.max(s, axis=0, keepdims=True)
        state[n]["m"] = mc if state[n]["m"] is None else jnp.maximum(state[n]["m"], mc)

    def pass_b(n, c):
        i, e = items[n]
        p = jnp.exp2(s_ref[n % 2, c] - state[n]["m"]).astype(BF)
        a = _dot(vtb_ref[e, :, c * qb:(c + 1) * qb], p)
        state[n]["acc"] = a if state[n]["acc"] is None else state[n]["acc"] + a
        if c == i:
            acc = state[n]["acc"]
            state[n]["o"] = acc[0:DH] / acc[DH:DH + 1]
            if e == 1:
                o = jnp.concatenate([state[n - 1]["o"], state[n]["o"]], axis=0).T
                o_ref[0, i * qb:(i + 1) * qb, :] = o.astype(BF)

    for n in range(len(items) + 1):
        na = items[n][0] + 1 if n < len(items) else 0
        nb = items[n - 1][0] + 1 if n >= 1 else 0
        for c in range(max(na, nb)):
            if c < na:
                pass_a(n, c)
            if c < nb:
                pass_b(n - 1, c)

    if tail:
        q0 = nfull * qb
        qtl_ref[...] = jnp.zeros(qtl_ref.shape, BF)
        qtl_ref[:, :, 0:tail] = qt_ref[0, :, :, q0:l]
        qrow = qtl_ref[...].reshape(kw, LANES).astype(F32).T[0:tail]
        lane = lax.broadcasted_iota(jnp.int32, (tail, kw), 1)
        la = lax.broadcasted_iota(jnp.int32, (tail, AUG), 1)
        vis = (lax.broadcasted_iota(jnp.int32, (tail, lp), 1) - lax.broadcasted_iota(jnp.int32, (tail, lp), 0)) <= q0
        outs = []
        for e in range(2):
            ccol = _col_from_row(cf[e:e + 1, q0:q0 + LANES], LANES)[0:tail]
            slots = _decay_slots(la, 6 * e, _split3_f32(ccol))
            qa = jnp.concatenate([jnp.where((lane >= e * DH) & (lane < (e + 1) * DH), qrow, 0.0).astype(BF), slots],
                                 axis=1)
            s = jnp.where(vis, _dot(qa, kat_ref[...]), NEG_INF)
            p = jnp.exp2(s - jnp.max(s, axis=1, keepdims=True)).astype(BF)
            acc = _dot_nt(p, vtb_ref[e])
            outs.append(acc[:, 0:DH] / acc[:, DH:DH + 1])
        o_ref[0, q0:l, :] = jnp.concatenate(outs, axis=1).astype(BF)


def _attn_prompt(qt, kt, vt, cf, *, qb=256):
    b, _, _, l = qt.shape
    lp = cf.shape[3]
    assert l % qb < LANES and lp % LANES == 0 and lp >= l
    pair = lambda: pl.BlockSpec((1, 2, DH, l), lambda i, j: (i, j, 0, 0))
    return pl.pallas_call(
        functools.partial(_attn_prompt_kernel, l=l, qb=qb),
        grid=(b, H // 2),
        in_specs=[pair(), pair(), pair(), pl.BlockSpec((1, 1, 2, lp), lambda i, j: (i, j, 0, 0))],
        out_specs=pl.BlockSpec((1, l, LANES), lambda i, j: (i, 0, j)),
        out_shape=jax.ShapeDtypeStruct((b, l, D), BF),
        scratch_shapes=[pltpu.VMEM((lp, 2 * DH + AUG), BF), pltpu.VMEM((2 * DH + AUG, lp), BF),
                        pltpu.VMEM((2, DH + AUG, lp), BF), pltpu.VMEM((2, DH, LANES), BF),
                        pltpu.VMEM((2 * DH, lp), F32), pltpu.VMEM((2, l // qb, qb, qb), F32)],
        compiler_params=_cparams(("arbitrary", "arbitrary")),
        name="attn_prompt",
    )(qt, kt, vt, cf)


def _attn_sample_kernel(pt_ref, q_ref, kn_ref, vn_ref, lfn_ref, *refs, pp):
    k_refs = refs[:pp]
    v_refs = refs[pp:2 * pp]
    lf_refs = refs[2 * pp:3 * pp]
    o_ref = refs[3 * pp]
    m_ref, l_ref, carry_ref, qb_ref, acc_ref = refs[3 * pp + 1:]
    g = pl.program_id(1)

    @pl.when(g == 0)
    def _():
        q = q_ref[0]
        lane0 = lax.broadcasted_iota(jnp.int32, (DH, PAGE), 1) == 0
        for h in range(H):
            qb_ref[h] = jnp.broadcast_to(_col_from_row(q[h:h + 1, :], DH), (DH, PAGE))
            vcol = _col_from_row(vn_ref[0, h:h + 1, :], DH)
            acc_ref[h] = jnp.where(lane0, jnp.broadcast_to(vcol, (DH, PAGE)), 0.0)
        m_ref[...] = jnp.sum(q * kn_ref[0], axis=1, keepdims=True)
        l_ref[...] = jnp.ones((H, 1), F32)
        carry_ref[...] = lfn_ref[0]

    r = lax.broadcasted_iota(jnp.int32, (PAGE, PAGE), 0)
    c = lax.broadcasted_iota(jnp.int32, (PAGE, PAGE), 1)
    later = (r > c).astype(BF)

    lf_all = jnp.concatenate([lf_refs[i][0] for i in range(pp)], axis=0)
    within = _dot3_l(lf_all, later)
    carry = carry_ref[...]
    qb = qb_ref[...]
    scores = []
    for i in range(pp):
        w_i = within[i * H:(i + 1) * H]
        scores.append(jnp.sum(k_refs[i][0] * qb, axis=1) + (w_i + carry))
        carry = carry + (w_i[:, 0:1] + lf_refs[i][0][:, 0:1])
    carry_ref[...] = carry

    m_old = m_ref[...]
    m_new = m_old
    for s in scores:
        m_new = jnp.maximum(m_new, jnp.max(s, axis=1, keepdims=True))
    alpha = jnp.exp(m_old - m_new)
    psum = jnp.zeros((H, 1), F32)
    acc = acc_ref[...] * jnp.broadcast_to(alpha, (H, PAGE))[:, None, :]
    for i in range(pp):
        p = jnp.exp(scores[i] - m_new)
        psum = psum + jnp.sum(p, axis=1, keepdims=True)
        acc = acc + p[:, None, :] * v_refs[i][0]
    acc_ref[...] = acc
    l_ref[...] = l_ref[...] * alpha + psum
    m_ref[...] = m_new

    @pl.when(g == pl.num_programs(1) - 1)
    def _():
        o_ref[0] = (jnp.sum(acc_ref[...], axis=2) / l_ref[...]).astype(o_ref.dtype)


def _attn_sample(page_table, q, k_new, v_new, lf_new, cache_kt, cache_vt, cache_lft, *, pp=8):
    n, npages = page_table.shape
    assert npages % pp == 0
    last = npages - 1

    def kv_spec(i):
        return pl.BlockSpec((1, H, DH, PAGE), lambda b, g, pt: (pt[b, last - (g * pp + i)], 0, 0, 0))

    def lf_spec(i):
        return pl.BlockSpec((1, H, PAGE), lambda b, g, pt: (pt[b, last - (g * pp + i)], 0, 0))

    per_seq = lambda w: pl.BlockSpec((1, H, w), lambda b, g, pt: (b, 0, 0))
    grid_spec = pltpu.PrefetchScalarGridSpec(
        num_scalar_prefetch=1,
        grid=(n, npages // pp),
        in_specs=[per_seq(DH), per_seq(DH), per_seq(DH), per_seq(1)]
        + [kv_spec(i) for i in range(pp)] + [kv_spec(i) for i in range(pp)] + [lf_spec(i) for i in range(pp)],
        out_specs=per_seq(DH),
        scratch_shapes=[pltpu.VMEM((H, 1), F32), pltpu.VMEM((H, 1), F32), pltpu.VMEM((H, 1), F32),
                        pltpu.VMEM((H, DH, PAGE), F32), pltpu.VMEM((H, DH, PAGE), F32)],
    )
    return pl.pallas_call(
        functools.partial(_attn_sample_kernel, pp=pp),
        grid_spec=grid_spec,
        out_shape=jax.ShapeDtypeStruct((n, H, DH), BF),
        compiler_params=_cparams(("arbitrary", "arbitrary")),
        name="attn_sample",
    )(page_table, q, k_new, v_new, lf_new, *([cache_kt] * pp), *([cache_vt] * pp), *([cache_lft] * pp))


def kernel(x_prompt, x_sample, state_conv, cache_k, cache_v, cache_logf, page_table, meta_tokens, a_norm_g, a_w_in, a_b_in, a_w_dw, a_b_dw, a_ln_g, a_ln_b, a_w_out, a_b_out, kv_norm_g, w_kvf, b_f, k_norm_g, b_norm_g, b_w_q, q_norm_g, b_w_o, mlp_norm_g, mlp_w_up, mlp_w_down):
    nb, seq, _ = x_prompt.shape
    ns = x_sample.shape[0]
    l = NMETA + seq
    t = nb * l
    row = lambda x: x.reshape(1, -1).astype(F32)

    w_in = a_w_in[0].astype(BF)
    b_in = row(a_b_in[0])
    w_dw = a_w_dw[0]
    w_dw_rep = jnp.repeat(w_dw, 8, axis=0)
    w_out = a_w_out[0].astype(BF)
    w_kvf_t = w_kvf.T.astype(BF)
    w_kt, w_vt, w_ft = w_kvf_t[:D], w_kvf_t[D:2 * D], w_kvf_t[2 * D:]
    w_q = b_w_q[0].astype(BF)
    w_o = b_w_o[0].astype(BF)
    w_up = mlp_w_up.astype(BF)
    w_dn = mlp_w_down.astype(BF)
    grp = jnp.arange(ONES_BLK) // DH
    ones_blk = (grp[:, None] == grp[None, :]).astype(BF)
    zero_b = jnp.zeros((1, D), F32)

    conv_w = (row(a_norm_g[0]), w_in, b_in)
    conv_post = (row(a_b_dw[0]), row(a_ln_g[0]), row(a_ln_b[0]))
    mlp0 = (w_out, row(a_b_out[0]), row(mlp_norm_g[0]), w_up[0], w_dn[0])
    mlp1 = (w_o, zero_b, row(mlp_norm_g[1]), w_up[1], w_dn[1])
    norms = (row(kv_norm_g), row(b_norm_g[0]))

    meta = jnp.broadcast_to(meta_tokens[None].astype(x_prompt.dtype), (nb, NMETA, D))
    h0 = jnp.concatenate([meta, x_prompt], axis=1)
    o0, tail = _conv_front(h0, *conv_w, w_dw_rep, *conv_post)
    conv_state_prompt = tail[None, :, CTX - (CW - 1):, :]
    h1 = _proj_mlp(h0.reshape(t, D), o0.reshape(t, D), *mlp0, tm=688)
    kt_p, vt_p, lft_p, qt_p = _kvq_seq(
        h1.reshape(nb, l, D), *norms, w_kt, w_vt, w_ft, b_f.reshape(H, 1), b_w_q[0].T.astype(BF),
        k_norm_g.reshape(DH, 1), q_norm_g[0].reshape(DH, 1))
    lp = -(-l // LANES) * LANES
    c_t = _cumsum_seq(jnp.pad(lft_p, ((0, 0), (0, 0), (0, lp - l))))
    o1 = _attn_prompt(qt_p, kt_p, vt_p, c_t.reshape(nb, H // 2, 2, lp))
    h2 = _proj_mlp(h1, o1.reshape(t, D), *mlp1, tm=688)
    y_prompt = h2.reshape(nb, l, D)[:, NMETA:]

    xs = x_sample.reshape(ns, D)
    st = jnp.transpose(state_conv[0], (1, 0, 2))
    o0s, u_s = _conv_sample(xs, st, *conv_w, w_dw, *conv_post)
    conv_state_sample = jnp.transpose(jnp.concatenate([st[1:], u_s[None]], axis=0), (1, 0, 2))[None]
    h1s = _proj_mlp(xs, o0s, *mlp0, tm=ns)
    k_s, v_s, lf_s, q_s = _kvq_rows(h1s, *norms, w_kt, w_vt, w_ft, row(b_f), w_q,
                                    row(jnp.tile(k_norm_g, H)), row(jnp.tile(q_norm_g[0], H)), ones_blk)
    o1s = _attn_sample(page_table, q_s.reshape(ns, H, DH), k_s.reshape(ns, H, DH), v_s.reshape(ns, H, DH),
                       lf_s.reshape(ns, H, 1),
                       jnp.transpose(cache_k, (0, 2, 3, 1)), jnp.transpose(cache_v, (0, 2, 3, 1)),
                       jnp.transpose(cache_logf, (0, 2, 1)))
    h2s = _proj_mlp(h1s, o1s.reshape(ns, D), *mlp1, tm=ns)

    return (y_prompt, h2s.reshape(ns, 1, D), conv_state_prompt,
            jnp.transpose(kt_p, (0, 3, 1, 2)), jnp.transpose(vt_p, (0, 3, 1, 2)), jnp.transpose(lft_p, (0, 2, 1)),
            conv_state_sample, k_s.reshape(ns, 1, H, DH), v_s.reshape(ns, 1, H, DH), lf_s.reshape(ns, 1, H))
```

```python
import functools

import jax
import jax.numpy as jnp
from jax import lax
from jax.experimental import pallas as pl
from jax.experimental.pallas import tpu as pltpu

F32 = jnp.float32
BF = jnp.bfloat16

D = 1024
H = 16
DH = 64
DFF = 4096
CW = 31
NMETA = 16
PAGE = 128
EPS = 1e-6
NEG_INF = -1e30
CTX = 32
LANES = 128
ONES_BLK = 256
AUG = 16
LOG2E = 1.4426950408889634
VMEM_LIMIT = 56 * 1024 * 1024


def _cparams(sem):
    return pltpu.CompilerParams(dimension_semantics=sem, vmem_limit_bytes=VMEM_LIMIT)


def _const_spec(shape):
    return pl.BlockSpec(shape, lambda *_: (0,) * len(shape), pipeline_mode=pl.Buffered(1))


def _dot(a, b):
    return jnp.dot(a, b, preferred_element_type=F32)


def _dot_nt(a, b):
    return lax.dot_general(a, b, (((1,), (1,)), ((), ())), preferred_element_type=F32)


def _rms_scale(x):
    return lax.rsqrt(jnp.mean(x * x, axis=-1, keepdims=True) + EPS)


def _split3(x):
    a1 = x.astype(BF)
    r1 = x - a1.astype(F32)
    a2 = r1.astype(BF)
    return a1, a2, (r1 - a2.astype(F32)).astype(BF)


def _dot3_l(x, m):
    a1, a2, a3 = _split3(x)
    return _dot(a1, m) + _dot(a2, m) + _dot(a3, m)


def _col_from_row(row, n):
    eye = lax.broadcasted_iota(jnp.int32, (n, n), 0) == lax.broadcasted_iota(jnp.int32, (n, n), 1)
    return jnp.sum(jnp.where(eye, jnp.broadcast_to(row, (n, n)), 0.0), axis=1, keepdims=True)


def _conv_tail(y, bdw, lng, lnb):
    y = y + bdw
    mu = jnp.mean(y, axis=-1, keepdims=True)
    yc = y - mu
    yn = yc * lax.rsqrt(jnp.mean(yc * yc, axis=-1, keepdims=True) + EPS)
    yn = yn * lng + lnb
    return yn * jax.nn.sigmoid(yn)


def _glu_in(x, g, win_ref, bin_ref):
    hn = (x * _rms_scale(x) * g).astype(BF)
    a = _dot(hn, win_ref[:, :D]) + bin_ref[:, :D]
    gate = _dot(hn, win_ref[:, D:]) + bin_ref[:, D:]
    return a * jax.nn.sigmoid(gate)


def _conv_front_kernel(x_ref, g_ref, win_ref, bin_ref, wb_ref, bdw_ref, lng_ref, lnb_ref,
                       o_ref, tail_ref, s_ref, sh_ref, *, ts, ch):
    t = pl.program_id(1)
    n_sh = sh_ref.shape[1]

    @pl.when(t == 0)
    def _():
        s_ref[0:CTX, :] = jnp.zeros((CTX, D), F32)

    @pl.when(t > 0)
    def _():
        s_ref[0:CTX, :] = s_ref[ts:ts + CTX, :]

    s_ref[CTX:CTX + ts, :] = _glu_in(x_ref[0], g_ref[...], win_ref, bin_ref)
    tail_ref[0] = s_ref[ts:ts + CTX, :]
    for r in range(1, 8):
        sh_ref[r - 1] = s_ref[r:r + n_sh, :]

    bdw = bdw_ref[...]
    lng = lng_ref[...]
    lnb = lnb_ref[...]

    def aligned(x, m):
        return x if isinstance(x, int) else pl.multiple_of(x, m)

    def taps(c):
        acc = jnp.zeros((ch // 8, 8, D), F32)
        for j in range(CW):
            off = CTX - (CW - 1) + j
            src = s_ref if off % 8 == 0 else sh_ref.at[off % 8 - 1]
            win = src[pl.ds(aligned(c * ch + off // 8 * 8, 8), ch), :]
            acc = acc + win.reshape(ch // 8, 8, D) * wb_ref[8 * j:8 * j + 8, :][None]
        return acc.reshape(ch, D)

    def finish(c, y):
        o_ref[0, pl.ds(aligned(c * ch, ch), ch), :] = _conv_tail(y, bdw, lng, lnb).astype(BF)

    def body(c, y_prev):
        y = taps(c)
        finish(c - 1, y_prev)
        return y

    n_ch = ts // ch
    finish(n_ch - 1, lax.fori_loop(1, n_ch, body, taps(0)))


def _conv_front(h, g, win, b_in, wb, bdw, lng, lnb, *, ts=688, ch=16):
    b, l, _ = h.shape
    assert l % ts == 0 and ts % ch == 0 and ch % 8 == 0
    row = lambda n: _const_spec((1, n))
    return pl.pallas_call(
        functools.partial(_conv_front_kernel, ts=ts, ch=ch),
        grid=(b, l // ts),
        in_specs=[
            pl.BlockSpec((1, ts, D), lambda i, t: (i, t, 0)),
            row(D), _const_spec((D, 2 * D)), row(2 * D), _const_spec((CW * 8, D)), row(D), row(D), row(D),
        ],
        out_specs=[
            pl.BlockSpec((1, ts, D), lambda i, t: (i, t, 0)),
            pl.BlockSpec((1, CTX, D), lambda i, t: (i, 0, 0)),
        ],
        out_shape=[
            jax.ShapeDtypeStruct((b, l, D), BF),
            jax.ShapeDtypeStruct((b, CTX, D), F32),
        ],
        scratch_shapes=[pltpu.VMEM((ts + CTX, D), F32), pltpu.VMEM((7, ts + CTX - 8, D), F32)],
        compiler_params=_cparams(("arbitrary", "arbitrary")),
        name="conv_front",
    )(h, g, win, b_in, wb, bdw, lng, lnb)


def _conv_sample_kernel(x_ref, st_ref, g_ref, win_ref, bin_ref, wdw_ref, bdw_ref, lng_ref, lnb_ref,
                        o_ref, u_ref):
    u = _glu_in(x_ref[...], g_ref[...], win_ref, bin_ref)
    u_ref[...] = u
    y = u * wdw_ref[CW - 1:CW, :]
    for j in range(CW - 1):
        y = y + st_ref[j] * wdw_ref[j:j + 1, :]
    o_ref[...] = _conv_tail(y, bdw_ref[...], lng_ref[...], lnb_ref[...]).astype(BF)


def _conv_sample(x, state, g, win, b_in, wdw, bdw, lng, lnb):
    n = x.shape[0]
    return pl.pallas_call(
        _conv_sample_kernel,
        out_shape=[jax.ShapeDtypeStruct((n, D), BF), jax.ShapeDtypeStruct((n, D), F32)],
        compiler_params=pltpu.CompilerParams(vmem_limit_bytes=VMEM_LIMIT),
        name="conv_sample",
    )(x, state, g, win, b_in, wdw, bdw, lng, lnb)


def _proj_mlp_kernel(h_ref, o_ref, wp_ref, bp_ref, g_ref, wup_ref, wdn_ref, out_ref, *, fc):
    x = h_ref[...] + _dot(o_ref[...], wp_ref[...]) + bp_ref[...]
    hn = (x * _rms_scale(x) * g_ref[...]).astype(BF)
    acc = x
    for c in range(DFF // fc):
        hu = _dot(hn, wup_ref[:, c * fc:(c + 1) * fc])
        a = jnp.square(jnp.maximum(hu, 0.0)).astype(BF)
        acc = acc + _dot(a, wdn_ref[c * fc:(c + 1) * fc, :])
    out_ref[...] = acc


def _proj_mlp(h, o, wp, bp, g, wup, wdn, *, tm, fc=1024):
    t = h.shape[0]
    assert t % tm == 0
    tile = lambda: pl.BlockSpec((tm, D), lambda i: (i, 0))
    return pl.pallas_call(
        functools.partial(_proj_mlp_kernel, fc=fc),
        grid=(t // tm,),
        in_specs=[tile(), tile(), _const_spec((D, D)), _const_spec((1, D)), _const_spec((1, D)),
                  _const_spec((D, DFF)), _const_spec((DFF, D))],
        out_specs=tile(),
        out_shape=jax.ShapeDtypeStruct((t, D), F32),
        compiler_params=_cparams(("arbitrary",)),
        name="proj_mlp",
    )(h, o, wp, bp, g, wup, wdn)


def _head_rms(x, g, ones_ref):
    sq = (x * x).astype(BF)
    parts = [_dot(sq[:, i * ONES_BLK:(i + 1) * ONES_BLK], ones_ref[...]) for i in range(x.shape[1] // ONES_BLK)]
    ms = (parts[0] if len(parts) == 1 else jnp.concatenate(parts, axis=1)) * (1.0 / DH)
    return x * lax.rsqrt(ms + EPS) * g


def _log_sigmoid(x):
    return jnp.minimum(x, 0.0) - jnp.log1p(jnp.exp(-jnp.abs(x)))


def _kvq_rows_kernel(h_ref, kvg_ref, bg_ref, wkt_ref, wvt_ref, wft_ref, bf_ref, wq_ref, kg_ref, qg_ref, ones_ref,
                     k_ref, v_ref, lf_ref, q_ref):
    h = h_ref[...]
    hs = h * _rms_scale(h)
    hn_kv = (hs * kvg_ref[...]).astype(BF)
    hn_q = (hs * bg_ref[...]).astype(BF)
    k_ref[...] = _head_rms(_dot_nt(hn_kv, wkt_ref[...]), kg_ref[...], ones_ref)
    v_ref[...] = _dot_nt(hn_kv, wvt_ref[...])
    lf_ref[...] = _log_sigmoid(_dot_nt(hn_kv, wft_ref[...]) + bf_ref[...])
    q_ref[...] = _head_rms(_dot(hn_q, wq_ref[...]), qg_ref[...], ones_ref) * (DH ** -0.5)


def _kvq_rows(h, kvg, bg, wkt, wvt, wft, bf_row, wq, kg, qg, ones_blk):
    n = h.shape[0]
    return pl.pallas_call(
        _kvq_rows_kernel,
        out_shape=[jax.ShapeDtypeStruct((n, D), F32), jax.ShapeDtypeStruct((n, D), F32),
                   jax.ShapeDtypeStruct((n, H), F32), jax.ShapeDtypeStruct((n, D), F32)],
        compiler_params=pltpu.CompilerParams(vmem_limit_bytes=VMEM_LIMIT),
        name="kvq_rows",
    )(h, kvg, bg, wkt, wvt, wft, bf_row, wq, kg, qg, ones_blk)


def _head_rms_t(xt, g_col):
    return xt * lax.rsqrt(jnp.mean(xt * xt, axis=1, keepdims=True) + EPS) * g_col[None]


def _kvq_seq_kernel(h_ref, kvg_ref, bg_ref, wkt_ref, wvt_ref, wft_ref, bfc_ref, wqt_ref, kgc_ref, qgc_ref,
                    kt_ref, vt_ref, lft_ref, qt_ref, hkv_ref, hq_ref, *, hg, rc):
    j = pl.program_id(1)
    l = hkv_ref.shape[0]

    @pl.when(j == 0)
    def _():
        def body(c, carry):
            r0 = pl.multiple_of(c * rc, rc)
            h = h_ref[0, pl.ds(r0, rc), :]
            hs = h * _rms_scale(h)
            hkv_ref[pl.ds(r0, rc), :] = (hs * kvg_ref[...]).astype(BF)
            hq_ref[pl.ds(r0, rc), :] = (hs * bg_ref[...]).astype(BF)
            return carry

        lax.fori_loop(0, l // rc, body, 0)
        lft_ref[0] = _log_sigmoid(_dot_nt(wft_ref[...], hkv_ref[...]) + bfc_ref[...])

    hkv = hkv_ref[...]
    kt_ref[0] = _head_rms_t(_dot_nt(wkt_ref[...], hkv).reshape(hg, DH, l), kgc_ref[...])
    vt_ref[0] = _dot_nt(wvt_ref[...], hkv).reshape(hg, DH, l)
    qt = _head_rms_t(_dot_nt(wqt_ref[...], hq_ref[...]).reshape(hg, DH, l), qgc_ref[...])
    qt_ref[0] = (qt * (DH ** -0.5 * LOG2E)).astype(BF)


def _kvq_seq(h, kvg, bg, wkt, wvt, wft, bf_col, wqt, kg_col, qg_col, *, hg=4, rc=688):
    b, l, _ = h.shape
    w = hg * DH
    assert l % rc == 0 and rc % 16 == 0
    row = lambda: _const_spec((1, D))
    heads = lambda: pl.BlockSpec((1, hg, DH, l), lambda i, j: (i, j, 0, 0))
    wrows = lambda: pl.BlockSpec((w, D), lambda i, j: (j, 0))
    return pl.pallas_call(
        functools.partial(_kvq_seq_kernel, hg=hg, rc=rc),
        grid=(b, H // hg),
        in_specs=[
            pl.BlockSpec((1, l, D), lambda i, j: (i, 0, 0), pipeline_mode=pl.Buffered(1)),
            row(), row(), wrows(), wrows(), _const_spec((H, D)), _const_spec((H, 1)), wrows(),
            _const_spec((DH, 1)), _const_spec((DH, 1)),
        ],
        out_specs=[heads(), heads(), pl.BlockSpec((1, H, l), lambda i, j: (i, 0, 0)), heads()],
        out_shape=[
            jax.ShapeDtypeStruct((b, H, DH, l), F32), jax.ShapeDtypeStruct((b, H, DH, l), F32),
            jax.ShapeDtypeStruct((b, H, l), F32), jax.ShapeDtypeStruct((b, H, DH, l), BF),
        ],
        scratch_shapes=[pltpu.VMEM((l, D), BF), pltpu.VMEM((l, D), BF)],
        compiler_params=_cparams(("arbitrary", "arbitrary")),
        name="kvq_seq",
    )(h, kvg, bg, wkt, wvt, wft, bf_col, wqt, kg_col, qg_col)


def _cumsum_kernel(x_ref, c_ref, *, nblk):
    r = lax.broadcasted_iota(jnp.int32, (LANES, LANES), 0)
    c = lax.broadcasted_iota(jnp.int32, (LANES, LANES), 1)
    upper = (r <= c).astype(BF)
    carry = jnp.zeros((H, 1), F32)
    for b in range(nblk):
        cb = _dot3_l(x_ref[0, :, b * LANES:(b + 1) * LANES], upper) + carry
        c_ref[0, :, b * LANES:(b + 1) * LANES] = cb
        carry = cb[:, LANES - 1:LANES]


def _cumsum_seq(xt):
    b, h, lp = xt.shape
    return pl.pallas_call(
        functools.partial(_cumsum_kernel, nblk=lp // LANES),
        grid=(b,),
        in_specs=[pl.BlockSpec((1, h, lp), lambda i: (i, 0, 0))],
        out_specs=pl.BlockSpec((1, h, lp), lambda i: (i, 0, 0)),
        out_shape=jax.ShapeDtypeStruct((b, h, lp), F32),
        compiler_params=_cparams(("arbitrary",)),
        name="cumsum_seq",
    )(xt)


def _split3_f32(x):
    return [p.astype(F32) for p in _split3(x)]


def _decay_slots(idx, base, pieces):
    out = jnp.where((idx >= base) & (idx < base + 3), 1.0, 0.0)
    for i, p in enumerate(pieces):
        out = jnp.where(idx == base + 3 + i, p, out)
    return out.astype(BF)


def _attn_prompt_kernel(qt_ref, kt_ref, vt_ref, cf_ref, o_ref, ka_ref, kat_ref, vtb_ref, qtl_ref, tr_ref, s_ref,
                        *, l, qb):
    lp = ka_ref.shape[0]
    nfull, tail = l // qb, l % qb
    kw = 2 * DH
    cf = cf_ref[0, 0] * LOG2E

    tr_ref[:, l:lp] = jnp.zeros((kw, lp - l), F32)
    tr_ref[:, 0:l] = kt_ref[0].reshape(kw, l)
    ri = lax.broadcasted_iota(jnp.int32, (AUG, lp), 0)
    kaug = jnp.zeros((AUG, lp), F32)
    for e in range(2):
        kaug = jnp.where((ri >= 6 * e + 3) & (ri < 6 * e + 6), 1.0, kaug)
        for i, p in enumerate(_split3_f32(cf[e:e + 1, :])):
            kaug = jnp.where(ri == 6 * e + i, -p, kaug)
    kat_ref[0:kw, :] = tr_ref[...].astype(BF)
    kat_ref[kw:kw + AUG, :] = kaug.astype(BF)
    ka_ref[:, 0:kw] = tr_ref[...].T.astype(BF)
    kaug_pad = jnp.concatenate([kaug, jnp.zeros((kw - AUG, lp), F32)], axis=0)
    ka_ref[:, kw:kw + AUG] = kaug_pad.T[:, 0:AUG].astype(BF)
    vtb_ref[:, 0:DH, l:lp] = jnp.zeros((2, DH, lp - l), BF)
    vtb_ref[:, 0:DH, 0:l] = vt_ref[0].astype(BF)
    vtb_ref[:, DH:DH + AUG, :] = jnp.broadcast_to(
        jnp.where(lax.broadcasted_iota(jnp.int32, (AUG, lp), 0) == 0, 1.0, 0.0).astype(BF)[None], (2, AUG, lp))

    rq = lax.broadcasted_iota(jnp.int32, (AUG, qb), 0)
    zq = jnp.zeros((DH, qb), BF)
    tri = lax.broadcasted_iota(jnp.int32, (qb, qb), 0) <= lax.broadcasted_iota(jnp.int32, (qb, qb), 1)

    def q_aug(i, e):
        q0 = i * qb
        qt = qt_ref[0, e, :, q0:q0 + qb]
        slots = _decay_slots(rq, 6 * e, _split3_f32(cf[e:e + 1, q0:q0 + qb]))
        return jnp.concatenate([qt if e == 0 else zq, qt if e == 1 else zq, slots], axis=0)

    items = [(i, e) for i in range(nfull) for e in range(2)]
    state = {}

    def pass_a(n, c):
        i, e = items[n]
        if c == 0:
            state[n] = dict(qa=q_aug(i, e), m=None, acc=None)
        s = _dot(ka_ref[c * qb:(c + 1) * qb, :], state[n]["qa"])
        if c == i:
            s = jnp.where(tri, s, NEG_INF)
        s_ref[n % 2, c] = s
        mc = jnp.max(s, axis=0, keepdims=True)
        state[n]["m"] = mc if state[n]["m"] is None else jnp.maximum(state[n]["m"], mc)

    def pass_b(n, c):
        i, e = items[n]
        p = jnp.exp2(s_ref[n % 2, c] - state[n]["m"]).astype(BF)
        a = _dot(vtb_ref[e, :, c * qb:(c + 1) * qb], p)
        state[n]["acc"] = a if state[n]["acc"] is None else state[n]["acc"] + a
        if c == i:
            acc = state[n]["acc"]
            state[n]["o"] = acc[0:DH] / acc[DH:DH + 1]
            if e == 1:
                o = jnp.concatenate([state[n - 1]["o"], state[n]["o"]], axis=0).T
                o_ref[0, i * qb:(i + 1) * qb, :] = o.astype(BF)

    for n in range(len(items) + 1):
        na = items[n][0] + 1 if n < len(items) else 0
        nb = items[n - 1][0] + 1 if n >= 1 else 0
        for c in range(max(na, nb)):
            if c < na:
                pass_a(n, c)
            if c < nb:
                pass_b(n - 1, c)

    if tail:
        q0 = nfull * qb
        qtl_ref[...] = jnp.zeros(qtl_ref.shape, BF)
        qtl_ref[:, :, 0:tail] = qt_ref[0, :, :, q0:l]
        qrow = qtl_ref[...].reshape(kw, LANES).astype(F32).T[0:tail]
        lane = lax.broadcasted_iota(jnp.int32, (tail, kw), 1)
        la = lax.broadcasted_iota(jnp.int32, (tail, AUG), 1)
        vis = (lax.broadcasted_iota(jnp.int32, (tail, lp), 1) - lax.broadcasted_iota(jnp.int32, (tail, lp), 0)) <= q0
        outs = []
        for e in range(2):
            ccol = _col_from_row(cf[e:e + 1, q0:q0 + LANES], LANES)[0:tail]
            slots = _decay_slots(la, 6 * e, _split3_f32(ccol))
            qa = jnp.concatenate([jnp.where((lane >= e * DH) & (lane < (e + 1) * DH), qrow, 0.0).astype(BF), slots],
                                 axis=1)
            s = jnp.where(vis, _dot(qa, kat_ref[...]), NEG_INF)
            p = jnp.exp2(s - jnp.max(s, axis=1, keepdims=True)).astype(BF)
            acc = _dot_nt(p, vtb_ref[e])
            outs.append(acc[:, 0:DH] / acc[:, DH:DH + 1])
        o_ref[0, q0:l, :] = jnp.concatenate(outs, axis=1).astype(BF)


def _attn_prompt(qt, kt, vt, cf, *, qb=256):
    b, _, _, l = qt.shape
    lp = cf.shape[3]
    assert l % qb < LANES and lp % LANES == 0 and lp >= l
    pair = lambda: pl.BlockSpec((1, 2, DH, l), lambda i, j: (i, j, 0, 0))
    return pl.pallas_call(
        functools.partial(_attn_prompt_kernel, l=l, qb=qb),
        grid=(b, H // 2),
        in_specs=[pair(), pair(), pair(), pl.BlockSpec((1, 1, 2, lp), lambda i, j: (i, j, 0, 0))],
        out_specs=pl.BlockSpec((1, l, LANES), lambda i, j: (i, 0, j)),
        out_shape=jax.ShapeDtypeStruct((b, l, D), BF),
        scratch_shapes=[pltpu.VMEM((lp, 2 * DH + AUG), BF), pltpu.VMEM((2 * DH + AUG, lp), BF),
                        pltpu.VMEM((2, DH + AUG, lp), BF), pltpu.VMEM((2, DH, LANES), BF),
                        pltpu.VMEM((2 * DH, lp), F32), pltpu.VMEM((2, l // qb, qb, qb), F32)],
        compiler_params=_cparams(("arbitrary", "arbitrary")),
        name="attn_prompt",
    )(qt, kt, vt, cf)


def _attn_sample_kernel(pt_ref, q_ref, kn_ref, vn_ref, lfn_ref, *refs, pp):
    k_refs = refs[:pp]
    v_refs = refs[pp:2 * pp]
    lf_refs = refs[2 * pp:3 * pp]
    o_ref = refs[3 * pp]
    m_ref, l_ref, carry_ref, qbd_ref, acc_ref = refs[3 * pp + 1:]
    g = pl.program_id(1)
    own = lax.broadcasted_iota(jnp.int32, (H, D), 1) // DH == lax.broadcasted_iota(jnp.int32, (H, D), 0)

    @pl.when(g == 0)
    def _():
        qbd = jnp.where(own, jnp.broadcast_to(q_ref[0], (H, D)), 0.0)
        qbd_ref[...] = qbd.astype(BF)
        m_ref[...] = jnp.sum(qbd * kn_ref[0], axis=1, keepdims=True)
        l_ref[...] = jnp.ones((H, 1), F32)
        carry_ref[...] = lfn_ref[0]
        acc_ref[...] = jnp.where(own, jnp.broadcast_to(vn_ref[0], (H, D)), 0.0)

    r = lax.broadcasted_iota(jnp.int32, (PAGE, PAGE), 0)
    c = lax.broadcasted_iota(jnp.int32, (PAGE, PAGE), 1)
    later = (r > c).astype(BF)

    lf_all = jnp.concatenate([lf_refs[i][0] for i in range(pp)], axis=0)
    within = _dot3_l(lf_all, later)
    carry = carry_ref[...]
    qbd = qbd_ref[...]
    scores = []
    for i in range(pp):
        w_i = within[i * H:(i + 1) * H]
        scores.append(_dot(qbd, k_refs[i][0].reshape(D, PAGE).astype(BF)) + (w_i + carry))
        carry = carry + (w_i[:, 0:1] + lf_refs[i][0][:, 0:1])
    carry_ref[...] = carry

    m_old = m_ref[...]
    m_new = m_old
    for s in scores:
        m_new = jnp.maximum(m_new, jnp.max(s, axis=1, keepdims=True))
    alpha = jnp.exp(m_old - m_new)
    psum = jnp.zeros((H, 1), F32)
    acc = acc_ref[...] * alpha
    for i in range(pp):
        p = jnp.exp(scores[i] - m_new)
        psum = psum + jnp.sum(p, axis=1, keepdims=True)
        acc = acc + _dot_nt(p.astype(BF), v_refs[i][0].reshape(D, PAGE).astype(BF))
    acc_ref[...] = acc
    l_ref[...] = l_ref[...] * alpha + psum
    m_ref[...] = m_new

    @pl.when(g == pl.num_programs(1) - 1)
    def _():
        o = jnp.sum(jnp.where(own, acc_ref[...] / l_ref[...], 0.0), axis=0, keepdims=True)
        o_ref[0] = o.astype(o_ref.dtype)


def _attn_sample(page_table, q, k_new, v_new, lf_new, cache_kt, cache_vt, cache_lft, *, pp=16):
    n, npages = page_table.shape
    assert npages % pp == 0
    last = npages - 1

    def kv_spec(i):
        return pl.BlockSpec((1, H, DH, PAGE), lambda b, g, pt: (pt[b, last - (g * pp + i)], 0, 0, 0))

    def lf_spec(i):
        return pl.BlockSpec((1, H, PAGE), lambda b, g, pt: (pt[b, last - (g * pp + i)], 0, 0))

    row = lambda: pl.BlockSpec((1, 1, D), lambda b, g, pt: (b, 0, 0))
    grid_spec = pltpu.PrefetchScalarGridSpec(
        num_scalar_prefetch=1,
        grid=(n, npages // pp),
        in_specs=[row(), row(), row(), pl.BlockSpec((1, H, 1), lambda b, g, pt: (b, 0, 0))]
        + [kv_spec(i) for i in range(pp)] + [kv_spec(i) for i in range(pp)] + [lf_spec(i) for i in range(pp)],
        out_specs=row(),
        scratch_shapes=[pltpu.VMEM((H, 1), F32), pltpu.VMEM((H, 1), F32), pltpu.VMEM((H, 1), F32),
                        pltpu.VMEM((H, D), BF), pltpu.VMEM((H, D), F32)],
    )
    return pl.pallas_call(
        functools.partial(_attn_sample_kernel, pp=pp),
        grid_spec=grid_spec,
        out_shape=jax.ShapeDtypeStruct((n, 1, D), BF),
        compiler_params=_cparams(("arbitrary", "arbitrary")),
        name="attn_sample",
    )(page_table, q, k_new, v_new, lf_new, *([cache_kt] * pp), *([cache_vt] * pp), *([cache_lft] * pp))


def kernel(x_prompt, x_sample, state_conv, cache_k, cache_v, cache_logf, page_table, meta_tokens, a_norm_g, a_w_in, a_b_in, a_w_dw, a_b_dw, a_ln_g, a_ln_b, a_w_out, a_b_out, kv_norm_g, w_kvf, b_f, k_norm_g, b_norm_g, b_w_q, q_norm_g, b_w_o, mlp_norm_g, mlp_w_up, mlp_w_down):
    nb, seq, _ = x_prompt.shape
    ns = x_sample.shape[0]
    l = NMETA + seq
    t = nb * l
    row = lambda x: x.reshape(1, -1).astype(F32)

    w_in = a_w_in[0].astype(BF)
    b_in = row(a_b_in[0])
    w_dw = a_w_dw[0]
    w_dw_rep = jnp.repeat(w_dw, 8, axis=0)
    w_out = a_w_out[0].astype(BF)
    w_kvf_t = w_kvf.T.astype(BF)
    w_kt, w_vt, w_ft = w_kvf_t[:D], w_kvf_t[D:2 * D], w_kvf_t[2 * D:]
    w_q = b_w_q[0].astype(BF)
    w_o = b_w_o[0].astype(BF)
    w_up = mlp_w_up.astype(BF)
    w_dn = mlp_w_down.astype(BF)
    grp = jnp.arange(ONES_BLK) // DH
    ones_blk = (grp[:, None] == grp[None, :]).astype(BF)
    zero_b = jnp.zeros((1, D), F32)

    conv_w = (row(a_norm_g[0]), w_in, b_in)
    conv_post = (row(a_b_dw[0]), row(a_ln_g[0]), row(a_ln_b[0]))
    mlp0 = (w_out, row(a_b_out[0]), row(mlp_norm_g[0]), w_up[0], w_dn[0])
    mlp1 = (w_o, zero_b, row(mlp_norm_g[1]), w_up[1], w_dn[1])
    norms = (row(kv_norm_g), row(b_norm_g[0]))

    meta = jnp.broadcast_to(meta_tokens[None].astype(x_prompt.dtype), (nb, NMETA, D))
    h0 = jnp.concatenate([meta, x_prompt], axis=1)
    o0, tail = _conv_front(h0, *conv_w, w_dw_rep, *conv_post)
    conv_state_prompt = tail[None, :, CTX - (CW - 1):, :]
    h1 = _proj_mlp(h0.reshape(t, D), o0.reshape(t, D), *mlp0, tm=688)
    kt_p, vt_p, lft_p, qt_p = _kvq_seq(
        h1.reshape(nb, l, D), *norms, w_kt, w_vt, w_ft, b_f.reshape(H, 1), b_w_q[0].T.astype(BF),
        k_norm_g.reshape(DH, 1), q_norm_g[0].reshape(DH, 1))
    lp = -(-l // LANES) * LANES
    c_t = _cumsum_seq(jnp.pad(lft_p, ((0, 0), (0, 0), (0, lp - l))))
    o1 = _attn_prompt(qt_p, kt_p, vt_p, c_t.reshape(nb, H // 2, 2, lp))
    h2 = _proj_mlp(h1, o1.reshape(t, D), *mlp1, tm=688)
    y_prompt = h2.reshape(nb, l, D)[:, NMETA:]

    xs = x_sample.reshape(ns, D)
    st = jnp.transpose(state_conv[0], (1, 0, 2))
    o0s, u_s = _conv_sample(xs, st, *conv_w, w_dw, *conv_post)
    conv_state_sample = jnp.transpose(jnp.concatenate([st[1:], u_s[None]], axis=0), (1, 0, 2))[None]
    h1s = _proj_mlp(xs, o0s, *mlp0, tm=ns)
    k_s, v_s, lf_s, q_s = _kvq_rows(h1s, *norms, w_kt, w_vt, w_ft, row(b_f), w_q,
                                    row(jnp.tile(k_norm_g, H)), row(jnp.tile(q_norm_g[0], H)), ones_blk)
    o1s = _attn_sample(page_table, q_s.reshape(ns, 1, D), k_s.reshape(ns, 1, D), v_s.reshape(ns, 1, D),
                       lf_s.reshape(ns, H, 1),
                       jnp.transpose(cache_k, (0, 2, 3, 1)), jnp.transpose(cache_v, (0, 2, 3, 1)),
                       jnp.transpose(cache_logf, (0, 2, 1)))
    h2s = _proj_mlp(h1s, o1s.reshape(ns, D), *mlp1, tm=ns)

    return (y_prompt, h2s.reshape(ns, 1, D), conv_state_prompt,
            jnp.transpose(kt_p, (0, 3, 1, 2)), jnp.transpose(vt_p, (0, 3, 1, 2)), jnp.transpose(lft_p, (0, 2, 1)),
            conv_state_sample, k_s.reshape(ns, 1, H, DH), v_s.reshape(ns, 1, H, DH), lf_s.reshape(ns, 1, H))
```
